```python
import jax, jax.numpy as jnp
from jax import lax
import numpy as np

D_MODEL = 1024
BATCH = 16
SEQ = 4096
DEPTH = 4

GRID_W = 64
CTX_LEN = 256
EPS = 1e-6

POOL_W = 256
POOL_GROUPS = 4
POOL_GC = POOL_W // POOL_GROUPS
POOL_WINDOWS = (2, 4, 8, 16)
MLA_HEADS = 8
MLA_NOPE = 64
MLA_ROPE = 32
MLA_V = 64
MLA_QK = MLA_NOPE + MLA_ROPE
Q_LORA = 256
KV_LORA = 128
Q_BLOCK = 128
ROPE_THETA = 10000.0
GLA_HEADS = 4
GLA_DK = 32
GLA_DV = 64
GLA_GATE_RANK = 16
GLA_GATE_NORM = 16.0
GLA_CHUNK = 64
MIX_W = POOL_W + MLA_HEADS * MLA_V + GLA_HEADS * GLA_DV
D_FF = 2816
CONV_W = 3

KEY_SIZES = (KV_LORA, MLA_ROPE, GLA_HEADS * GLA_DK, GLA_HEADS * GLA_DV, GLA_GATE_RANK, GLA_GATE_RANK)
QUERY_SIZES = (POOL_W, Q_LORA, GLA_HEADS * GLA_DK, GLA_HEADS * GLA_DV)
KEY_COLS = sum(KEY_SIZES)
IN_COLS = KEY_COLS + sum(QUERY_SIZES)

kernel_name = "hybrid_pool_mla_gla_diffusion_trunk"


def split_cols(z, sizes):
    out, o = [], 0
    for s in sizes:
        out.append(z[..., o:o + s])
        o += s
    return out


def rmsnorm(x, g):
    xf = x.astype(jnp.float32)
    y = xf * lax.rsqrt(jnp.mean(xf * xf, axis=-1, keepdims=True) + EPS)
    return y.astype(x.dtype) * g


def modulate(h, shift, scale):
    return h * (1.0 + scale) + shift


def rope_2d_tables(L):
    rows = L // GRID_W
    row = jnp.repeat(jnp.arange(rows), GRID_W).astype(jnp.float32)
    col = jnp.tile(jnp.arange(GRID_W), rows).astype(jnp.float32)
    half = MLA_ROPE // 2
    inv = ROPE_THETA ** (-jnp.arange(0, half, 2, dtype=jnp.float32) / half)
    ar = row[:, None] * inv
    ac = col[:, None] * inv
    ang = jnp.concatenate([ar, ar, ac, ac], axis=-1)
    return jnp.cos(ang), jnp.sin(ang)


def apply_rope(x, cos, sin):
    xf = x.astype(jnp.float32)
    xr = xf.reshape(xf.shape[:-1] + (2, 2, MLA_ROPE // 4))
    rot = jnp.stack([-xr[..., 1, :], xr[..., 0, :]], axis=-2).reshape(xf.shape)
    return (xf * cos + rot * sin).astype(x.dtype)


def multiscale_pool(u, w_pool, pool_scale):
    B, L, _ = u.shape
    P = jnp.pad(jnp.cumsum(u.astype(jnp.float32), axis=1), ((0, 0), (1, 0), (0, 0)))
    t = jnp.arange(L)
    means = []
    for gi, w in enumerate(POOL_WINDOWS):
        lo = jnp.clip(t - w // 2, 0, L)
        hi = jnp.clip(t - w // 2 + w, 0, L)
        Pg = P[..., gi * POOL_GC:(gi + 1) * POOL_GC]
        s = jnp.take(Pg, hi, axis=1) - jnp.take(Pg, lo, axis=1)
        means.append(s / (hi - lo).astype(jnp.float32)[None, :, None])
    pooled = jnp.stack(means, axis=2).astype(u.dtype)
    diff = pooled - u.reshape(B, L, POOL_GROUPS, POOL_GC)
    y = jnp.einsum('blgc,gcd->blgd', diff, w_pool).reshape(B, L, POOL_W)
    return y * pool_scale


def mla_softmax(qn, qr, kn, kr, v):
    s = jnp.einsum('bqhd,bkhd->bhqk', qn, kn) + jnp.einsum('bqhr,bkr->bhqk', qr, kr)
    p = jax.nn.softmax(s.astype(jnp.float32) * (MLA_QK ** -0.5), axis=-1)
    return jnp.einsum('bhqk,bkhd->bqhd', p.astype(v.dtype), v)


def mla_blockwise(qn, qr, kn, kr, v):
    B, L = qn.shape[:2]
    nb = L // Q_BLOCK
    to_blocks = lambda a: a.reshape((B, nb, Q_BLOCK) + a.shape[2:]).swapaxes(0, 1)
    out = lax.map(lambda qs: mla_softmax(qs[0], qs[1], kn, kr, v), (to_blocks(qn), to_blocks(qr)))
    return out.swapaxes(0, 1).reshape(B, L, MLA_HEADS, MLA_V)


def gla_gate(lr, w, b):
    B, L, _ = lr.shape
    g = jax.nn.log_sigmoid((lr @ w + b).astype(jnp.float32)) / GLA_GATE_NORM
    return g.reshape(B, L, GLA_HEADS, GLA_DK)


def gla_scan(k, v, g, s0, q=None):
    B, L, H, _ = k.shape
    n = L // GLA_CHUNK
    chunks = lambda a: a.astype(jnp.float32).reshape(B, n, GLA_CHUNK, H, a.shape[-1])
    kc, vc = chunks(k), chunks(v)
    b = jnp.cumsum(chunks(g), axis=2)
    b_last = b[:, :, -1:]
    kv = jnp.einsum('bnchd,bnchv->nbhdv', kc * jnp.exp(b_last - b), vc)
    decay = jnp.exp(b_last[:, :, 0]).swapaxes(0, 1)

    def step(s, inp):
        d, kvn = inp
        return d[..., None] * s + kvn, s

    s_fin, s_in = lax.scan(step, s0, (decay, kv))
    if q is None:
        return None, s_fin
    qc = chunks(q) * (GLA_DK ** -0.5)
    mid = b[:, :, GLA_CHUNK // 2:GLA_CHUNK // 2 + 1]
    tri = jnp.tril(jnp.ones((GLA_CHUNK, GLA_CHUNK), jnp.float32))
    att = jnp.einsum('bnihd,bnjhd->bnhij', qc * jnp.exp(b - mid), kc * jnp.exp(mid - b)) * tri
    o = (jnp.einsum('bnhij,bnjhv->bnihv', att, vc)
         + jnp.einsum('bnihd,nbhdv->bnihv', qc * jnp.exp(b), s_in))
    return o.reshape(B, L, H, GLA_DV).astype(v.dtype), s_fin


def token_mix(h_l, h_c, w_in, q_norm, w_uq, kv_norm, w_ukv, w_gk_f, b_gk_f, w_gk_b, b_gk_b,
              gla_norm, w_pool, pool_scale, cos, sin, ctx_out):
    B, L, _ = h_l.shape
    Lc = h_c.shape[1]
    flip = lambda a: a[:, ::-1]
    z_l = h_l @ w_in
    z_c = h_c @ (w_in if ctx_out else w_in[:, :KEY_COLS])
    ckv_l, kr_l, gk_l, gv_l, lrf_l, lrb_l, pool_l, cq_l, gq_l, og_l = split_cols(z_l, KEY_SIZES + QUERY_SIZES)
    ckv_c, kr_c, gk_c, gv_c, lrf_c, lrb_c = split_cols(z_c[..., :KEY_COLS], KEY_SIZES)

    def mla_kv(ckv):
        ukv = (rmsnorm(ckv, kv_norm) @ w_ukv).reshape(B, -1, MLA_HEADS, MLA_NOPE + MLA_V)
        return ukv[..., :MLA_NOPE], ukv[..., MLA_NOPE:]

    def mla_q(cq):
        uq = (rmsnorm(cq, q_norm) @ w_uq).reshape(B, -1, MLA_HEADS, MLA_QK)
        return uq[..., :MLA_NOPE], uq[..., MLA_NOPE:]

    kn_l, v_l = mla_kv(ckv_l)
    kn_c, v_c = mla_kv(ckv_c)
    kr_l = apply_rope(kr_l, cos, sin)
    kn_all = jnp.concatenate([kn_c, kn_l], axis=1)
    kr_all = jnp.concatenate([kr_c, kr_l], axis=1)
    v_all = jnp.concatenate([v_c, v_l], axis=1)
    qn_l, qr_l = mla_q(cq_l)
    qr_l = apply_rope(qr_l, cos[:, None, :], sin[:, None, :])
    att_l = mla_blockwise(qn_l, qr_l, kn_all, kr_all, v_all).reshape(B, L, MLA_HEADS * MLA_V)

    heads = lambda a, d: a.reshape(a.shape[0], a.shape[1], GLA_HEADS, d)
    s_zero = jnp.zeros((B, GLA_HEADS, GLA_DK, GLA_DV), jnp.float32)
    gq_c = heads(split_cols(z_c[..., KEY_COLS:], QUERY_SIZES)[2], GLA_DK) if ctx_out else None
    kc_, vc_ = heads(gk_c, GLA_DK), heads(gv_c, GLA_DV)
    o_cf, s_cf = gla_scan(kc_, vc_, gla_gate(lrf_c, w_gk_f, b_gk_f), s_zero, gq_c)
    o_cb, s_cb = gla_scan(flip(kc_), flip(vc_), flip(gla_gate(lrb_c, w_gk_b, b_gk_b)), s_zero,
                          None if gq_c is None else flip(gq_c))
    ql_, kl_, vl_ = heads(gq_l, GLA_DK), heads(gk_l, GLA_DK), heads(gv_l, GLA_DV)
    o_lf, _ = gla_scan(kl_, vl_, gla_gate(lrf_l, w_gk_f, b_gk_f), s_cf, ql_)
    o_lb, _ = gla_scan(flip(kl_), flip(vl_), flip(gla_gate(lrb_l, w_gk_b, b_gk_b)), s_cb, flip(ql_))

    def gla_out(o, og):
        y = rmsnorm(o, gla_norm) * jax.nn.silu(heads(og, GLA_DV))
        return y.reshape(y.shape[0], y.shape[1], GLA_HEADS * GLA_DV)

    gla_l = gla_out(o_lf + flip(o_lb), og_l)
    y_l = jnp.concatenate([multiscale_pool(pool_l, w_pool, pool_scale), att_l, gla_l], axis=-1)
    if not ctx_out:
        return y_l, None

    pool_c, cq_c, _, og_c = split_cols(z_c[..., KEY_COLS:], QUERY_SIZES)
    qn_c, qr_c = mla_q(cq_c)
    att_c = mla_softmax(qn_c, qr_c, kn_c, kr_c, v_c).reshape(B, Lc, MLA_HEADS * MLA_V)
    gla_c = gla_out(o_cf + flip(o_cb), og_c)
    y_c = jnp.concatenate([multiscale_pool(pool_c, w_pool, pool_scale), att_c, gla_c], axis=-1)
    return y_l, y_c


def conv_ffn(h, w_up, conv_w, conv_b, w_down):
    u, g = split_cols(h @ w_up, (D_FF, D_FF))
    gp = jnp.pad(g, ((0, 0), (1, 1), (0, 0)))
    g = gp[:, :-2] * conv_w[0] + gp[:, 1:-1] * conv_w[1] + gp[:, 2:] * conv_w[2] + conv_b
    return (jax.nn.silu(g) * u) @ w_down


def setup_inputs(seed: int = 0) -> dict:
    key = jax.random.key(seed)
    ks = jax.random.split(key, 32)
    nrm = lambda k, shape, scale: jax.random.normal(k, shape, jnp.float32) * scale
    gain = lambda k, shape: 1.0 + nrm(k, shape, 0.05)
    D, L = DEPTH, D_MODEL
    return {
        "x": nrm(ks[0], (BATCH, SEQ, D_MODEL), 1.0),
        "c": nrm(ks[1], (BATCH, D_MODEL), 1.0),
        "ctx": nrm(ks[2], (BATCH, CTX_LEN, D_MODEL), 1.0),
        "c_ctx": nrm(ks[3], (D_MODEL,), 1.0),
        "w_ada": nrm(ks[4], (D, D_MODEL, 6 * D_MODEL), 0.1 * D_MODEL ** -0.5),
        "b_ada": nrm(ks[5], (D, 6 * D_MODEL), 0.02),
        "norm1": gain(ks[6], (D, D_MODEL)),
        "norm2": gain(ks[7], (D, D_MODEL)),
        "w_in": nrm(ks[8], (D, D_MODEL, IN_COLS), D_MODEL ** -0.5),
        "q_norm": gain(ks[9], (D, Q_LORA)),
        "w_uq": nrm(ks[10], (D, Q_LORA, MLA_HEADS * MLA_QK), Q_LORA ** -0.5),
        "kv_norm": gain(ks[11], (D, KV_LORA)),
        "w_ukv": nrm(ks[12], (D, KV_LORA, MLA_HEADS * (MLA_NOPE + MLA_V)), KV_LORA ** -0.5),
        "w_gk_f": nrm(ks[13], (D, GLA_GATE_RANK, GLA_HEADS * GLA_DK), GLA_GATE_RANK ** -0.5),
        "b_gk_f": nrm(ks[14], (D, GLA_HEADS * GLA_DK), 0.1),
        "w_gk_b": nrm(ks[15], (D, GLA_GATE_RANK, GLA_HEADS * GLA_DK), GLA_GATE_RANK ** -0.5),
        "b_gk_b": nrm(ks[16], (D, GLA_HEADS * GLA_DK), 0.1),
        "gla_norm": gain(ks[17], (D, GLA_DV)),
        "w_pool": nrm(ks[18], (D, POOL_GROUPS, POOL_GC, POOL_GC), POOL_GC ** -0.5),
        "pool_scale": gain(ks[19], (D, POOL_W)),
        "w_o": nrm(ks[20], (D, MIX_W, D_MODEL), MIX_W ** -0.5),
        "w_up": nrm(ks[21], (D, D_MODEL, 2 * D_FF), D_MODEL ** -0.5),
        "conv_w": nrm(ks[22], (D, CONV_W, D_FF), CONV_W ** -0.5),
        "conv_b": nrm(ks[23], (D, D_FF), 0.02),
        "w_down": nrm(ks[24], (D, D_FF, D_MODEL), D_FF ** -0.5),
        "norm_f": gain(ks[25], (D_MODEL,)),
    }


def reference(x, c, ctx, c_ctx, w_ada, b_ada, norm1, norm2, w_in, q_norm, w_uq, kv_norm, w_ukv,
              w_gk_f, b_gk_f, w_gk_b, b_gk_b, gla_norm, w_pool, pool_scale, w_o, w_up, conv_w,
              conv_b, w_down, norm_f):
    B, L, Dm = x.shape
    cos, sin = rope_2d_tables(L)
    silu_c = jax.nn.silu(c)
    silu_cc = jax.nn.silu(c_ctx)
    xc = ctx
    for i in range(DEPTH):
        last = i == DEPTH - 1
        m_l = (silu_c @ w_ada[i] + b_ada[i]).reshape(B, 1, 6, Dm)
        n_c = 2 if last else 6
        m_c = (silu_cc @ w_ada[i][:, :n_c * Dm] + b_ada[i][:n_c * Dm]).reshape(n_c, Dm)
        h_l = modulate(rmsnorm(x, norm1[i]), m_l[:, :, 0], m_l[:, :, 1])
        h_c = modulate(rmsnorm(xc, norm1[i]), m_c[0], m_c[1])
        y_l, y_c = token_mix(h_l, h_c, w_in[i], q_norm[i], w_uq[i], kv_norm[i], w_ukv[i],
                             w_gk_f[i], b_gk_f[i], w_gk_b[i], b_gk_b[i], gla_norm[i],
                             w_pool[i], pool_scale[i], cos, sin, not last)
        x = x + m_l[:, :, 2] * (y_l @ w_o[i])
        x = x + m_l[:, :, 5] * conv_ffn(modulate(rmsnorm(x, norm2[i]), m_l[:, :, 3], m_l[:, :, 4]),
                                        w_up[i], conv_w[i], conv_b[i], w_down[i])
        if not last:
            xc = xc + m_c[2] * (y_c @ w_o[i])
            xc = xc + m_c[5] * conv_ffn(modulate(rmsnorm(xc, norm2[i]), m_c[3], m_c[4]),
                                        w_up[i], conv_w[i], conv_b[i], w_down[i])
    return rmsnorm(x, norm_f)
```

```python
import functools

import numpy as np
import jax
import jax.numpy as jnp
from jax import lax
from jax.experimental import pallas as pl
from jax.experimental.pallas import tpu as pltpu

F32 = jnp.float32
BF16 = jnp.bfloat16

EPS = 1e-6
GRID_W = 64
POOL_W = 256
POOL_GROUPS = 4
POOL_GC = POOL_W // POOL_GROUPS
POOL_WINDOWS = (2, 4, 8, 16)
MLA_HEADS = 8
MLA_NOPE = 64
MLA_ROPE = 32
MLA_V = 64
MLA_QK = MLA_NOPE + MLA_ROPE
Q_LORA = 256
KV_LORA = 128
ROPE_THETA = 10000.0
GLA_HEADS = 4
GLA_DK = 32
GLA_DV = 64
GLA_GATE_RANK = 16
GLA_GATE_NORM = 16.0
GLA_CHUNK = 64
GLA_KW = GLA_HEADS * GLA_DK
GLA_VW = GLA_HEADS * GLA_DV
D_FF = 2816

TILE = 256
HALO = 8
LANES = 128
HEAD_PAD = LANES
MIX_W = POOL_W + MLA_HEADS * MLA_V + GLA_VW
FF_CHUNKS = 2
VMEM_LIMIT = 56 * 1024 * 1024

C_CKV = 0
C_CQ = C_CKV + KV_LORA
C_POOL = C_CQ + Q_LORA
C_GQK = C_POOL + POOL_W
C_GV = C_GQK + 2 * GLA_KW
C_OG = C_GV + GLA_VW
C_MISC_A = C_OG + GLA_VW
C_MISC_B = C_MISC_A + LANES
IN_COLS_P = C_MISC_B + LANES


def _dot(a, b):
    return jnp.dot(a, b, preferred_element_type=F32)


def _dot_nt(a, b):
    return lax.dot_general(a, b, (((1,), (1,)), ((), ())), preferred_element_type=F32)


def _dot_tn(a, b):
    return lax.dot_general(a, b, (((0,), (0,)), ((), ())), preferred_element_type=F32)


def _dot_exact_rhs(a, m):
    a1 = a.astype(BF16)
    r = a - a1.astype(F32)
    a2 = r.astype(BF16)
    a3 = (r - a2.astype(F32)).astype(BF16)
    return _dot(a1, m) + _dot(a2, m) + _dot(a3, m)


def _dot_exact_lhs(m, a):
    a1 = a.astype(BF16)
    r = a - a1.astype(F32)
    a2 = r.astype(BF16)
    a3 = (r - a2.astype(F32)).astype(BF16)
    return _dot(m, a1) + _dot(m, a2) + _dot(m, a3)


def _rms(x, g):
    return x * lax.rsqrt(jnp.mean(x * x, axis=-1, keepdims=True) + EPS) * g


def _silu(x):
    return x * jax.nn.sigmoid(x)


def _params():
    return pltpu.CompilerParams(vmem_limit_bytes=VMEM_LIMIT)


def _const_spec(shape):
    nd = len(shape)
    return pl.BlockSpec(shape, lambda *_: (0,) * nd, pipeline_mode=pl.Buffered(1))


def _ada_kernel(c_ref, w_ref, b_ref, o_ref):
    s = _silu(c_ref[...]).astype(BF16)
    o_ref[0] = _dot(s, w_ref[0].astype(BF16)) + b_ref[0]


def _ada(cvec, w_ada, b_ada):
    depth, d, n = w_ada.shape
    bp = cvec.shape[0]
    tn = n // 4
    return pl.pallas_call(
        _ada_kernel,
        out_shape=jax.ShapeDtypeStruct((depth, bp, n), F32),
        grid=(depth, n // tn),
        in_specs=[
            pl.BlockSpec((bp, d), lambda i, j: (0, 0)),
            pl.BlockSpec((1, d, tn), lambda i, j: (i, 0, j)),
            pl.BlockSpec((1, 1, tn), lambda i, j: (i, 0, j)),
        ],
        out_specs=pl.BlockSpec((1, bp, tn), lambda i, j: (i, 0, j)),
        compiler_params=_params(),
        name="ada",
    )(cvec, w_ada, b_ada.reshape(depth, 1, n))


def _pre_kernel(x_ref, mod_ref, n1_ref, win_ref, qn_ref, wuq_ref, kvn_ref, wukv_ref, wg_ref, bg_ref, tab_ref,
                q_ref, k_ref, v_ref, pu_ref, gqk_ref, gg_ref, gv_ref, og_ref):
    x = x_ref[0]
    mod = mod_ref[0]
    h = _rms(x, n1_ref[...]) * (1.0 + mod[1:2]) + mod[0:1]
    z = _dot(h.astype(BF16), win_ref[...])
    pu_ref[0] = z[:, C_POOL:C_POOL + POOL_W]
    gqk_ref[0] = z[:, C_GQK:C_GQK + 2 * GLA_KW]
    gv_ref[0] = z[:, C_GV:C_GV + GLA_VW]
    og_ref[0] = z[:, C_OG:C_OG + GLA_VW]
    za = z[:, C_MISC_A:C_MISC_A + LANES]
    zb = z[:, C_MISC_B:C_MISC_B + LANES]
    tab = tab_ref[...]
    cosq, sinq = tab[:, 0:LANES], tab[:, LANES:2 * LANES]
    cosk, sink = tab[:, 2 * LANES:3 * LANES], tab[:, 3 * LANES:4 * LANES]

    a = _dot(za.astype(BF16), wg_ref[...]) + bg_ref[...]
    gg_ref[0] = (jnp.minimum(a, 0.0) - jnp.log1p(jnp.exp(-jnp.abs(a)))) * (1.0 / GLA_GATE_NORM)

    ckvn = _rms(z[:, C_CKV:C_CKV + KV_LORA], kvn_ref[...]).astype(BF16)
    ukv = _dot(ckvn, wukv_ref[...])
    kr = za * cosk + zb * sink
    kw = MLA_HEADS * HEAD_PAD
    for hd in range(MLA_HEADS):
        sl = slice(hd * HEAD_PAD, (hd + 1) * HEAD_PAD)
        k_ref[0, :, sl] = (ukv[:, sl] + kr).astype(BF16)
    v_ref[0] = ukv[:, kw:kw + MLA_HEADS * MLA_V].astype(BF16)

    cqn = _rms(z[:, C_CQ:C_CQ + Q_LORA], qn_ref[...]).astype(BF16)
    uq = _dot(cqn, wuq_ref[...])
    for hd in range(MLA_HEADS):
        sl = slice(hd * HEAD_PAD, (hd + 1) * HEAD_PAD)
        sr = slice(kw + hd * HEAD_PAD, kw + (hd + 1) * HEAD_PAD)
        q_ref[0, :, sl] = (uq[:, sl] * cosq + uq[:, sr] * sinq).astype(BF16)


def _pre(xa, mod, n1, win, qn, wuq, kvn, wukv, wg, bg, tab, *, n_ctx_tiles, ctx_row):
    b_, s_, d = xa.shape
    t_ = TILE
    nt = s_ // t_
    tok = lambda w: pl.BlockSpec((1, t_, w), lambda b, t: (b, t, 0))
    outs = [(MLA_HEADS * HEAD_PAD, BF16), (MLA_HEADS * HEAD_PAD, BF16), (MLA_HEADS * MLA_V, BF16),
            (POOL_W, F32), (2 * GLA_KW, F32), (2 * GLA_KW, F32), (GLA_VW, F32), (GLA_VW, F32)]
    return pl.pallas_call(
        _pre_kernel,
        out_shape=[jax.ShapeDtypeStruct((b_, s_, w), dt) for w, dt in outs],
        grid=(b_, nt),
        in_specs=[
            tok(d),
            pl.BlockSpec((1, 6, d), lambda b, t: (jnp.where(t < n_ctx_tiles, ctx_row, b), 0, 0)),
            _const_spec(n1.shape), _const_spec(win.shape), _const_spec(qn.shape), _const_spec(wuq.shape),
            _const_spec(kvn.shape), _const_spec(wukv.shape), _const_spec(wg.shape), _const_spec(bg.shape),
            pl.BlockSpec((t_, 4 * LANES), lambda b, t: (t, 0)),
        ],
        out_specs=[tok(w) for w, _ in outs],
        compiler_params=_params(),
        name="pre",
    )(xa, mod, n1, win, qn, wuq, kvn, wukv, wg, bg, tab)


def _attn_kernel(q_ref, k_ref, v_ref, o_ref, *, n_ctx_tiles, n_tiles, t0):
    t_ = TILE
    qi = pl.program_id(2) + t0
    n_kv = jnp.where(qi < n_ctx_tiles, n_ctx_tiles, n_tiles)
    res = []
    for hh in range(2):
        sl = slice(hh * HEAD_PAD, (hh + 1) * HEAD_PAD)
        q = q_ref[0, :, sl]

        def body(j, carry, sl=sl, q=q):
            m, l, acc = carry
            off = pl.multiple_of(j * t_, t_)
            kb = k_ref[0, pl.ds(off, t_), sl]
            vb = v_ref[0, pl.ds(off, t_), :]
            s = _dot_nt(q, kb)
            m_new = jnp.maximum(m, jnp.max(s, axis=-1, keepdims=True))
            alpha = jnp.exp(m - m_new)
            p = jnp.exp(s - m_new)
            l = alpha * l + jnp.sum(p, axis=-1, keepdims=True)
            acc = alpha * acc + _dot(p.astype(BF16), vb)
            return m_new, l, acc

        init = (jnp.full((t_, 1), -jnp.inf, F32), jnp.zeros((t_, 1), F32), jnp.zeros((t_, 2 * MLA_V), F32))
        _, l, acc = lax.fori_loop(0, n_kv, body, init)
        res.append(acc / l)
    lane = lax.broadcasted_iota(jnp.int32, (t_, 2 * MLA_V), 1)
    o_ref[0] = jnp.where(lane < MLA_V, res[0], res[1]).astype(BF16)


def _attn(q, k, v, *, n_ctx_tiles, t0):
    b_, s_, _ = q.shape
    t_ = TILE
    nt = s_ // t_
    pairs = MLA_HEADS // 2
    return pl.pallas_call(
        functools.partial(_attn_kernel, n_ctx_tiles=n_ctx_tiles, n_tiles=nt, t0=t0),
        out_shape=jax.ShapeDtypeStruct((b_, (nt - t0) * t_, MLA_HEADS * MLA_V), BF16),
        grid=(b_, pairs, nt - t0),
        in_specs=[
            pl.BlockSpec((1, t_, 2 * HEAD_PAD), lambda b, h, t: (b, t + t0, h)),
            pl.BlockSpec((1, s_, 2 * HEAD_PAD), lambda b, h, t: (b, 0, h)),
            pl.BlockSpec((1, s_, 2 * MLA_V), lambda b, h, t: (b, 0, h)),
        ],
        out_specs=pl.BlockSpec((1, t_, 2 * MLA_V), lambda b, h, t: (b, t, h)),
        compiler_params=_params(),
        name="attn",
    )(q, k, v)


def _gla_kernel(qk_ref, g_ref, v_ref, mst_ref, o_ref, st_ref, *, rev):
    t_ = TILE
    ch = GLA_CHUNK

    @pl.when(pl.program_id(1) == 0)
    def _():
        st_ref[...] = jnp.zeros_like(st_ref)

    q = qk_ref[0, :, 0:GLA_KW] * (GLA_DK ** -0.5)
    k = qk_ref[0, :, GLA_KW:2 * GLA_KW]
    v = v_ref[0].astype(BF16)
    cs = _dot_exact_lhs(mst_ref[...], g_ref[0])
    b, bl, mid = cs[0:t_], cs[t_:2 * t_], cs[2 * t_:3 * t_]
    kw = (k * jnp.exp(bl - b)).astype(BF16)
    qe = (q * jnp.exp(b)).astype(BF16)
    q2 = q * jnp.exp(b - mid)
    k2 = (k * jnp.exp(mid - b)).astype(BF16)

    ri = lax.broadcasted_iota(jnp.int32, (t_, t_), 0)
    ci = lax.broadcasted_iota(jnp.int32, (t_, t_), 1)
    same = (ri >> 6) == (ci >> 6)
    tri = (ri <= ci) if rev else (ri >= ci)
    lane_k = lax.broadcasted_iota(jnp.int32, (1, GLA_KW), 1)
    lane_v = lax.broadcasted_iota(jnp.int32, (1, GLA_VW), 1)
    o_intra = jnp.zeros((t_, GLA_VW), F32)
    for hd in range(GLA_HEADS):
        qh = jnp.where((lane_k >> 5) == hd, q2, 0.0).astype(BF16)
        att = _dot_nt(qh, k2)
        att = jnp.where(same, jnp.where(tri, att, 0.0), 0.0).astype(BF16)
        o_intra = o_intra + jnp.where((lane_v >> 6) == hd, _dot(att, v), 0.0)

    rs = lax.broadcasted_iota(jnp.int32, (GLA_VW, GLA_KW), 0)
    cs_ = lax.broadcasted_iota(jnp.int32, (GLA_VW, GLA_KW), 1)
    bd = (rs >> 6) == (cs_ >> 5)
    st = st_ref[...]
    order = range(t_ // ch - 1, -1, -1) if rev else range(t_ // ch)
    for c in order:
        rows = slice(c * ch, (c + 1) * ch)
        o_ref[0, rows, :] = o_intra[rows] + _dot_nt(qe[rows], st.astype(BF16))
        dec = jnp.exp(bl[c * ch:c * ch + 1])
        st = dec * st + jnp.where(bd, _dot_tn(v[rows], kw[rows]), 0.0)
    st_ref[...] = st


def _gla_consts(rev):
    t_, ch = TILE, GLA_CHUNK
    i = np.arange(t_)[:, None]
    j = np.arange(t_)[None, :]
    same = (i // ch) == (j // ch)
    if rev:
        cum = same & (j >= i)
        mid = same & ((j % ch) >= ch - 1 - ch // 2)
    else:
        cum = same & (j <= i)
        mid = same & ((j % ch) <= ch // 2)
    return jnp.asarray(np.concatenate([cum, same, mid], axis=0).astype(np.float32), dtype=BF16)


def _gla(gqk, gg, gv, *, n_ctx_tiles, rev):
    b_, s_, _ = gqk.shape
    t_ = TILE
    nt = s_ // t_
    if rev:
        tile = lambda t: jnp.where(t < n_ctx_tiles, n_ctx_tiles - 1 - t, nt - 1 - (t - n_ctx_tiles))
    else:
        tile = lambda t: t
    d = 1 if rev else 0
    mst = _gla_consts(rev)
    return pl.pallas_call(
        functools.partial(_gla_kernel, rev=rev),
        out_shape=jax.ShapeDtypeStruct((b_, s_, GLA_VW), F32),
        grid=(b_, nt),
        in_specs=[
            pl.BlockSpec((1, t_, 2 * GLA_KW), lambda b, t: (b, tile(t), 0)),
            pl.BlockSpec((1, t_, GLA_KW), lambda b, t: (b, tile(t), d)),
            pl.BlockSpec((1, t_, GLA_VW), lambda b, t: (b, tile(t), 0)),
            _const_spec(mst.shape),
        ],
        out_specs=pl.BlockSpec((1, t_, GLA_VW), lambda b, t: (b, tile(t), 0)),
        scratch_shapes=[pltpu.VMEM((GLA_VW, GLA_KW), F32)],
        compiler_params=_params(),
        name="gla_bwd" if rev else "gla_fwd",
    )(gqk, gg, gv, mst)


def _segment(t, n_ctx_tiles, n_tiles):
    in_ctx = t < n_ctx_tiles
    return jnp.where(in_ctx, t, t - n_ctx_tiles), jnp.where(in_ctx, n_ctx_tiles, n_tiles - n_ctx_tiles)


def _halo_specs(width, n_ctx_tiles, n_tiles, t0):
    per = TILE // HALO

    def prev(b, t):
        tt = t + t0
        ts, _ = _segment(tt, n_ctx_tiles, n_tiles)
        return (b, jnp.where(ts == 0, tt * per, tt * per - 1), 0)

    def nxt(b, t):
        tt = t + t0
        ts, n = _segment(tt, n_ctx_tiles, n_tiles)
        return (b, jnp.where(ts == n - 1, (tt + 1) * per - 1, (tt + 1) * per), 0)

    return pl.BlockSpec((1, HALO, width), prev), pl.BlockSpec((1, HALO, width), nxt)


def _post_kernel(x_ref, mod_ref, pu_ref, pp_ref, pn_ref, att_ref, of_ref, ob_ref, og_ref, gn_ref, wpool_ref,
                 ps_ref, wo_ref, ones_ref, o_ref, ue_ref, *, n_ctx_tiles, n_tiles, t0):
    t_ = TILE
    ts, nseg = _segment(pl.program_id(1) + t0, n_ctx_tiles, n_tiles)
    seg_len = nseg * t_

    ue_ref[0:HALO] = jnp.where(ts > 0, pp_ref[0], 0.0)
    ue_ref[HALO:HALO + t_] = pu_ref[0]
    ue_ref[HALO + t_:2 * HALO + t_] = jnp.where(ts < nseg - 1, pn_ref[0], 0.0)
    pos = ts * t_ + lax.broadcasted_iota(jnp.int32, (t_, 1), 0)
    lane = lax.broadcasted_iota(jnp.int32, (1, LANES), 1)

    def shifted(offsets, c0):
        acc = None
        for o in offsets:
            r = ue_ref[pl.ds(HALO + o, t_), c0:c0 + LANES]
            acc = r if acc is None else acc + r
        return acc

    def count(w):
        lo = pos - w // 2
        return (jnp.minimum(lo + w, seg_len) - jnp.maximum(lo, 0)).astype(F32)

    w2 = shifted((-1, 0), 0)
    w4 = w2 + shifted((-2, 1), 0)
    w8 = shifted(range(-4, 4), LANES)
    w16 = w8 + shifted(tuple(range(-8, -4)) + tuple(range(4, 8)), LANES)
    u = pu_ref[0]
    d01 = jnp.where(lane < POOL_GC, w2 / count(2), w4 / count(4)) - u[:, 0:LANES]
    d23 = jnp.where(lane < POOL_GC, w8 / count(8), w16 / count(16)) - u[:, LANES:2 * LANES]
    diff = jnp.concatenate([d01, d23], axis=1).astype(BF16)
    ypool = _dot(diff, wpool_ref[...]) * ps_ref[...]

    o = of_ref[0] + ob_ref[0]
    ms = _dot_exact_rhs(o * o, ones_ref[...]) * (1.0 / GLA_DV)
    yg = o * lax.rsqrt(ms + EPS) * gn_ref[...] * _silu(og_ref[0])

    ycat = jnp.concatenate([ypool.astype(BF16), att_ref[0], yg.astype(BF16)], axis=1)
    o_ref[0] = x_ref[0] + mod_ref[0][2:3] * _dot(ycat, wo_ref[...])


def _post(xa, mod, pu, att, of, ob, og, gn, wpool, ps, wo, ones, *, n_ctx_tiles, ctx_row, t0):
    b_, s_, d = xa.shape
    t_ = TILE
    nt = s_ // t_
    tok = lambda w: pl.BlockSpec((1, t_, w), lambda b, t: (b, t + t0, 0))
    hp, hn = _halo_specs(POOL_W, n_ctx_tiles, nt, t0)
    return pl.pallas_call(
        functools.partial(_post_kernel, n_ctx_tiles=n_ctx_tiles, n_tiles=nt, t0=t0),
        out_shape=jax.ShapeDtypeStruct((b_, (nt - t0) * t_, d), F32),
        grid=(b_, nt - t0),
        in_specs=[
            tok(d),
            pl.BlockSpec((1, 6, d), lambda b, t: (jnp.where(t + t0 < n_ctx_tiles, ctx_row, b), 0, 0)),
            tok(POOL_W), hp, hn,
            pl.BlockSpec((1, t_, MLA_HEADS * MLA_V), lambda b, t: (b, t, 0)),
            tok(GLA_VW), tok(GLA_VW), tok(GLA_VW),
            _const_spec(gn.shape), _const_spec(wpool.shape), _const_spec(ps.shape), _const_spec(wo.shape),
            _const_spec(ones.shape),
        ],
        out_specs=pl.BlockSpec((1, t_, d), lambda b, t: (b, t, 0)),
        scratch_shapes=[pltpu.VMEM((t_ + 2 * HALO, POOL_W), F32)],
        compiler_params=_params(),
        name="post",
    )(xa, mod, pu, pu, pu, att, of, ob, og, gn, wpool, ps, wo, ones)


def _ffn_kernel(x_ref, xp_ref, xn_ref, mod_ref, n2_ref, wup_ref, cw_ref, cb_ref, wdn_ref, nf_ref, o_ref, ge_ref,
                *, n_ctx_tiles, n_tiles, final):
    t_ = TILE
    fc = D_FF // FF_CHUNKS
    ts, nseg = _segment(pl.program_id(1), n_ctx_tiles, n_tiles)
    mod = mod_ref[0]
    n2 = n2_ref[...]

    def hmod(xx):
        return (_rms(xx, n2) * (1.0 + mod[4:5]) + mod[3:4]).astype(BF16)

    x = x_ref[0]
    h = hmod(x)
    hp = hmod(xp_ref[0])
    hn = hmod(xn_ref[0])
    y = jnp.zeros_like(x)
    for c in range(FF_CHUNKS):
        cu = slice(c * fc, (c + 1) * fc)
        cg = slice(D_FF + c * fc, D_FF + (c + 1) * fc)
        u = _dot(h, wup_ref[:, cu])
        g = _dot(h, wup_ref[:, cg])
        ge_ref[0:HALO] = jnp.where(ts > 0, _dot(hp, wup_ref[:, cg]), 0.0)
        ge_ref[HALO:HALO + t_] = g
        ge_ref[HALO + t_:2 * HALO + t_] = jnp.where(ts < nseg - 1, _dot(hn, wup_ref[:, cg]), 0.0)
        cw = cw_ref[:, cu]
        gc = (ge_ref[pl.ds(HALO - 1, t_), :] * cw[0:1] + g * cw[1:2]
              + ge_ref[pl.ds(HALO + 1, t_), :] * cw[2:3] + cb_ref[:, cu])
        y = y + _dot((_silu(gc) * u).astype(BF16), wdn_ref[cu, :])
    x2 = x + mod[5:6] * y
    o_ref[0] = _rms(x2, nf_ref[...]) if final else x2


def _ffn(x1, mod, n2, wup, cw, cb, wdn, nf, *, n_ctx_tiles, ctx_row, final):
    b_, s_, d = x1.shape
    t_ = TILE
    nt = s_ // t_
    hp, hn = _halo_specs(d, n_ctx_tiles, nt, 0)
    return pl.pallas_call(
        functools.partial(_ffn_kernel, n_ctx_tiles=n_ctx_tiles, n_tiles=nt, final=final),
        out_shape=jax.ShapeDtypeStruct((b_, s_, d), F32),
        grid=(b_, nt),
        in_specs=[
            pl.BlockSpec((1, t_, d), lambda b, t: (b, t, 0)), hp, hn,
            pl.BlockSpec((1, 6, d), lambda b, t: (jnp.where(t < n_ctx_tiles, ctx_row, b), 0, 0)),
            _const_spec(n2.shape), _const_spec(wup.shape), _const_spec(cw.shape), _const_spec(cb.shape),
            _const_spec(wdn.shape), _const_spec(nf.shape),
        ],
        out_specs=pl.BlockSpec((1, t_, d), lambda b, t: (b, t, 0)),
        scratch_shapes=[pltpu.VMEM((t_ + 2 * HALO, D_FF // FF_CHUNKS), F32)],
        compiler_params=_params(),
        name="ffn",
    )(x1, x1, x1, mod, n2, wup, cw, cb, wdn, nf)


def _rot_cols(w):
    w4 = w.reshape(w.shape[:-1] + (2, 2, MLA_ROPE // 4))
    return jnp.stack([-w4[..., 1, :], w4[..., 0, :]], axis=-2).reshape(w.shape)


def _prep_weights(w_in, w_uq, w_ukv, w_gk_f, b_gk_f, w_gk_b, b_gk_b, w_pool, gla_norm):
    depth, d, _ = w_in.shape
    z = lambda *s: jnp.zeros((depth,) + s, F32)
    o = 0
    parts = {}
    for name, size in (("ckv", KV_LORA), ("kr", MLA_ROPE), ("gk", GLA_KW), ("gv", GLA_VW),
                       ("lrf", GLA_GATE_RANK), ("lrb", GLA_GATE_RANK), ("pool", POOL_W), ("cq", Q_LORA),
                       ("gq", GLA_KW), ("og", GLA_VW)):
        parts[name] = w_in[..., o:o + size]
        o += size
    win = jnp.concatenate([
        parts["ckv"], parts["cq"], parts["pool"], parts["gq"], parts["gk"], parts["gv"], parts["og"],
        parts["lrf"], parts["lrb"], z(d, 32), parts["kr"], z(d, 32),
        z(d, 64), _rot_cols(parts["kr"]), z(d, 32)], axis=-1).astype(BF16)

    uq = w_uq.reshape(depth, Q_LORA, MLA_HEADS, MLA_QK)
    nope, rope = uq[..., :MLA_NOPE], uq[..., MLA_NOPE:]
    zq = lambda w: jnp.zeros((depth, Q_LORA, MLA_HEADS, w), F32)
    qa = jnp.concatenate([nope, rope, zq(32)], axis=-1).reshape(depth, Q_LORA, -1)
    qb = jnp.concatenate([zq(64), _rot_cols(rope), zq(32)], axis=-1).reshape(depth, Q_LORA, -1)
    wuq = jnp.concatenate([qa, qb], axis=-1).astype(BF16)

    ukv = w_ukv.reshape(depth, KV_LORA, MLA_HEADS, MLA_NOPE + MLA_V)
    kn = jnp.concatenate([ukv[..., :MLA_NOPE], jnp.zeros((depth, KV_LORA, MLA_HEADS, 64), F32)], axis=-1)
    wukv = jnp.concatenate([kn.reshape(depth, KV_LORA, -1), ukv[..., MLA_NOPE:].reshape(depth, KV_LORA, -1)],
                           axis=-1).astype(BF16)

    r = GLA_GATE_RANK
    wg = jnp.zeros((depth, LANES, 2 * GLA_KW), F32)
    wg = wg.at[:, 0:r, 0:GLA_KW].set(w_gk_f).at[:, r:2 * r, GLA_KW:].set(w_gk_b).astype(BF16)
    bg = jnp.concatenate([b_gk_f, b_gk_b], axis=-1)[:, None, :]

    wpool = jnp.zeros((depth, POOL_W, POOL_W), F32)
    for g in range(POOL_GROUPS):
        sl = slice(g * POOL_GC, (g + 1) * POOL_GC)
        wpool = wpool.at[:, sl, sl].set(w_pool[:, g])
    gn = jnp.tile(gla_norm, (1, GLA_HEADS))[:, None, :]
    return win, wuq, wukv, wg, bg, wpool.astype(BF16), gn


def _rope_tables(seq, ctx_len):
    rows = seq // GRID_W
    row = jnp.repeat(jnp.arange(rows), GRID_W).astype(F32)
    col = jnp.tile(jnp.arange(GRID_W), rows).astype(F32)
    half = MLA_ROPE // 2
    inv = ROPE_THETA ** (-jnp.arange(0, half, 2, dtype=F32) / half)
    ar = row[:, None] * inv
    ac = col[:, None] * inv
    ang = jnp.concatenate([ar, ar, ac, ac], axis=-1)
    cos = jnp.concatenate([jnp.ones((ctx_len, MLA_ROPE), F32), jnp.cos(ang)], axis=0)
    sin = jnp.concatenate([jnp.zeros((ctx_len, MLA_ROPE), F32), jnp.sin(ang)], axis=0)
    s_ = ctx_len + seq
    scale = MLA_QK ** -0.5
    z = lambda w: jnp.zeros((s_, w), F32)
    return jnp.concatenate([
        jnp.ones((s_, MLA_NOPE), F32) * scale, cos * scale, z(32),
        z(MLA_NOPE), sin * scale, z(32),
        z(MLA_NOPE), cos, z(32),
        z(MLA_NOPE), sin, z(32)], axis=-1)


def kernel(x, c, ctx, c_ctx, w_ada, b_ada, norm1, norm2, w_in, q_norm, w_uq, kv_norm, w_ukv, w_gk_f, b_gk_f,
           w_gk_b, b_gk_b, gla_norm, w_pool, pool_scale, w_o, w_up, conv_w, conv_b, w_down, norm_f):
    b_, seq, d = x.shape
    ctx_len = ctx.shape[1]
    depth = w_ada.shape[0]
    assert seq % TILE == 0 and ctx_len % TILE == 0 and seq % GRID_W == 0
    nc = ctx_len // TILE

    bp = -(-(b_ + 1) // 8) * 8
    cvec = jnp.concatenate([c, c_ctx[None, :], jnp.zeros((bp - b_ - 1, d), F32)], axis=0)
    mod_all = _ada(cvec, w_ada, b_ada).reshape(depth, bp, 6, d)

    win, wuq, wukv, wg, bg, wpool, gn = _prep_weights(w_in, w_uq, w_ukv, w_gk_f, b_gk_f, w_gk_b, b_gk_b,
                                                      w_pool, gla_norm)
    wo = w_o.astype(BF16)
    wup = w_up.astype(BF16)
    wdn = w_down.astype(BF16)
    tab = _rope_tables(seq, ctx_len)
    gi = np.arange(GLA_VW) // GLA_DV
    ones = jnp.asarray((gi[:, None] == gi[None, :]).astype(np.float32), dtype=BF16)

    xa = jnp.concatenate([ctx, x], axis=1)
    for i in range(depth):
        last = i == depth - 1
        t0 = nc if last else 0
        mod = mod_all[i]
        q, k, v, pu, gqk, gg, gv, og = _pre(xa, mod, norm1[i][None], win[i], q_norm[i][None], wuq[i],
                                            kv_norm[i][None], wukv[i], wg[i], bg[i], tab,
                                            n_ctx_tiles=nc, ctx_row=b_)
        att = _attn(q, k, v, n_ctx_tiles=nc, t0=t0)
        of = _gla(gqk, gg, gv, n_ctx_tiles=nc, rev=False)
        ob = _gla(gqk, gg, gv, n_ctx_tiles=nc, rev=True)
        x1 = _post(xa, mod, pu, att, of, ob, og, gn[i], wpool[i], pool_scale[i][None], wo[i], ones,
                   n_ctx_tiles=nc, ctx_row=b_, t0=t0)
        xa = _ffn(x1, mod, norm2[i][None], wup[i], conv_w[i], conv_b[i][None], wdn[i], norm_f[None],
                  n_ctx_tiles=0 if last else nc, ctx_row=b_, final=last)
    return xa
```

```python
import functools

import numpy as np
import jax
import jax.numpy as jnp
from jax import lax
from jax.experimental import pallas as pl
from jax.experimental.pallas import tpu as pltpu

F32 = jnp.float32
BF16 = jnp.bfloat16

EPS = 1e-6
LOG2E = 1.4426950408889634
GRID_W = 64
POOL_W = 256
POOL_GROUPS = 4
POOL_GC = POOL_W // POOL_GROUPS
POOL_WINDOWS = (2, 4, 8, 16)
MLA_HEADS = 8
MLA_NOPE = 64
MLA_ROPE = 32
MLA_V = 64
MLA_QK = MLA_NOPE + MLA_ROPE
Q_LORA = 256
KV_LORA = 128
ROPE_THETA = 10000.0
GLA_HEADS = 4
GLA_DK = 32
GLA_DV = 64
GLA_GATE_RANK = 16
GLA_GATE_NORM = 16.0
GLA_CHUNK = 64
GLA_KW = GLA_HEADS * GLA_DK
GLA_VW = GLA_HEADS * GLA_DV
D_FF = 2816

TILE = 256
HALO = 8
LANES = 128
HEAD_PAD = LANES
V_ROWS = 80
MIX_W = POOL_W + MLA_HEADS * MLA_V + GLA_VW
FF_CHUNKS = 2
SM_CHUNKS = 4
ATTN_UNROLL = 4
VMEM_LIMIT = 56 * 1024 * 1024

C_CKV = 0
C_CQ = C_CKV + KV_LORA
C_POOL = C_CQ + Q_LORA
C_GQK = C_POOL + POOL_W
C_GV = C_GQK + 2 * GLA_KW
C_OG = C_GV + GLA_VW
C_MISC_A = C_OG + GLA_VW
C_MISC_B = C_MISC_A + LANES
IN_COLS_P = C_MISC_B + LANES


def _dot(a, b):
    return jnp.dot(a, b, preferred_element_type=F32)


def _dot_nt(a, b):
    return lax.dot_general(a, b, (((1,), (1,)), ((), ())), preferred_element_type=F32)


def _dot_tn(a, b):
    return lax.dot_general(a, b, (((0,), (0,)), ((), ())), preferred_element_type=F32)


def _dot_exact_rhs(a, m):
    a1 = a.astype(BF16)
    r = a - a1.astype(F32)
    a2 = r.astype(BF16)
    a3 = (r - a2.astype(F32)).astype(BF16)
    return _dot(a1, m) + _dot(a2, m) + _dot(a3, m)


def _dot_exact_lhs(m, a):
    a1 = a.astype(BF16)
    r = a - a1.astype(F32)
    a2 = r.astype(BF16)
    a3 = (r - a2.astype(F32)).astype(BF16)
    return _dot(m, a1) + _dot(m, a2) + _dot(m, a3)


def _rms(x, g):
    return x * lax.rsqrt(jnp.mean(x * x, axis=-1, keepdims=True) + EPS) * g


def _silu(x):
    return x * jax.nn.sigmoid(x)


def _params():
    return pltpu.CompilerParams(vmem_limit_bytes=VMEM_LIMIT)


def _const_spec(shape):
    nd = len(shape)
    return pl.BlockSpec(shape, lambda *_: (0,) * nd, pipeline_mode=pl.Buffered(1))


def _ada_kernel(c_ref, w_ref, b_ref, o_ref):
    s = _silu(c_ref[...]).astype(BF16)
    o_ref[0] = _dot(s, w_ref[0].astype(BF16)) + b_ref[0]


def _ada(cvec, w_ada, b_ada):
    depth, d, n = w_ada.shape
    bp = cvec.shape[0]
    tn = n // 4
    return pl.pallas_call(
        _ada_kernel,
        out_shape=jax.ShapeDtypeStruct((depth, bp, n), F32),
        grid=(depth, n // tn),
        in_specs=[
            pl.BlockSpec((bp, d), lambda i, j: (0, 0)),
            pl.BlockSpec((1, d, tn), lambda i, j: (i, 0, j)),
            pl.BlockSpec((1, 1, tn), lambda i, j: (i, 0, j)),
        ],
        out_specs=pl.BlockSpec((1, bp, tn), lambda i, j: (i, 0, j)),
        compiler_params=_params(),
        name="ada",
    )(cvec, w_ada, b_ada.reshape(depth, 1, n))


def _pre_kernel(x_ref, mod_ref, n1_ref, win_ref, qn_ref, wuq_ref, kvn_ref, wuk_ref, wvt_ref, wg_ref, bg_ref, tab_ref,
                q_ref, k_ref, vt_ref, pu_ref, gqk_ref, gg_ref, gv_ref, og_ref):
    x = x_ref[0]
    mod = mod_ref[0]
    h = _rms(x, n1_ref[...]) * (1.0 + mod[1:2]) + mod[0:1]
    z = _dot(h.astype(BF16), win_ref[...])
    pu_ref[0] = z[:, C_POOL:C_POOL + POOL_W]
    gqk_ref[0] = z[:, C_GQK:C_GQK + 2 * GLA_KW]
    gv_ref[0] = z[:, C_GV:C_GV + GLA_VW]
    og_ref[0] = z[:, C_OG:C_OG + GLA_VW]
    za = z[:, C_MISC_A:C_MISC_A + LANES]
    zb = z[:, C_MISC_B:C_MISC_B + LANES]
    tab = tab_ref[...]
    cosq, sinq = tab[:, 0:LANES], tab[:, LANES:2 * LANES]
    cosk, sink = tab[:, 2 * LANES:3 * LANES], tab[:, 3 * LANES:4 * LANES]

    a = _dot(za.astype(BF16), wg_ref[...]) + bg_ref[...]
    gg_ref[0] = (jnp.minimum(a, 0.0) - jnp.log1p(jnp.exp(-jnp.abs(a)))) * (1.0 / GLA_GATE_NORM)

    ckvn = _rms(z[:, C_CKV:C_CKV + KV_LORA], kvn_ref[...]).astype(BF16)
    uk = _dot(ckvn, wuk_ref[...])
    kr = za * cosk + zb * sink
    kw = MLA_HEADS * HEAD_PAD
    for hd in range(MLA_HEADS):
        sl = slice(hd * HEAD_PAD, (hd + 1) * HEAD_PAD)
        k_ref[0, :, sl] = (uk[:, sl] + kr).astype(BF16)
    vt = _dot_nt(wvt_ref[...], ckvn)
    pad_row = lax.broadcasted_iota(jnp.int32, (V_ROWS - MLA_V, x.shape[0]), 0)
    pad = jnp.where(pad_row == 0, 1.0, 0.0).astype(BF16)
    for hd in range(MLA_HEADS):
        vt_ref[0, hd, 0:MLA_V, :] = vt[hd * MLA_V:(hd + 1) * MLA_V].astype(BF16)
        vt_ref[0, hd, MLA_V:V_ROWS, :] = pad

    cqn = _rms(z[:, C_CQ:C_CQ + Q_LORA], qn_ref[...]).astype(BF16)
    uq = _dot(cqn, wuq_ref[...])
    for hd in range(MLA_HEADS):
        sl = slice(hd * HEAD_PAD, (hd + 1) * HEAD_PAD)
        sr = slice(kw + hd * HEAD_PAD, kw + (hd + 1) * HEAD_PAD)
        q_ref[0, :, sl] = (uq[:, sl] * cosq + uq[:, sr] * sinq).astype(BF16)


def _pre(xa, mod, n1, win, qn, wuq, kvn, wuk, wvt, wg, bg, tab, *, n_ctx_tiles, ctx_row):
    b_, s_, d = xa.shape
    t_ = TILE
    nt = s_ // t_
    tok = lambda w: pl.BlockSpec((1, t_, w), lambda b, t: (b, t, 0))
    outs = [(MLA_HEADS * HEAD_PAD, BF16), (MLA_HEADS * HEAD_PAD, BF16), None,
            (POOL_W, F32), (2 * GLA_KW, F32), (2 * GLA_KW, F32), (GLA_VW, F32), (GLA_VW, F32)]
    vt_shape = jax.ShapeDtypeStruct((b_, MLA_HEADS, V_ROWS, s_), BF16)
    vt_spec = pl.BlockSpec((1, MLA_HEADS, V_ROWS, t_), lambda b, t: (b, 0, 0, t))
    return pl.pallas_call(
        _pre_kernel,
        out_shape=[vt_shape if o is None else jax.ShapeDtypeStruct((b_, s_, o[0]), o[1]) for o in outs],
        grid=(b_, nt),
        in_specs=[
            tok(d),
            pl.BlockSpec((1, 6, d), lambda b, t: (jnp.where(t < n_ctx_tiles, ctx_row, b), 0, 0)),
            _const_spec(n1.shape), _const_spec(win.shape), _const_spec(qn.shape), _const_spec(wuq.shape),
            _const_spec(kvn.shape), _const_spec(wuk.shape), _const_spec(wvt.shape), _const_spec(wg.shape),
            _const_spec(bg.shape),
            pl.BlockSpec((t_, 4 * LANES), lambda b, t: (t, 0)),
        ],
        out_specs=[vt_spec if o is None else tok(o[0]) for o in outs],
        compiler_params=_params(),
        name="pre",
    )(xa, mod, n1, win, qn, wuq, kvn, wuk, wvt, wg, bg, tab)


def _attn_kernel(q_ref, k_ref, vt_ref, o_ref, m_ref, al_ref, acc_ref, s_ref, p_ref, *, n_ctx_tiles, n_tiles, t0):
    t_ = TILE
    nh = MLA_HEADS
    rc = t_ // SM_CHUNKS
    qi = pl.program_id(1) + t0
    n_kv = jnp.where(qi < n_ctx_tiles, n_ctx_tiles, n_tiles)
    m_ref[...] = jnp.full(m_ref.shape, -jnp.inf, F32)
    al_ref[...] = jnp.ones(al_ref.shape, F32)
    acc_ref[...] = jnp.zeros(acc_ref.shape, F32)
    p_ref[...] = jnp.zeros(p_ref.shape, BF16)

    def scores(off, hd, slot):
        sl = slice(hd * HEAD_PAD, (hd + 1) * HEAD_PAD)
        s_ref[slot] = _dot_nt(k_ref[0, pl.ds(off, t_), sl], q_ref[0, :, sl])

    def softmax(hd, slot):
        mx = s_ref[slot, 0:rc, :]
        for c in range(1, SM_CHUNKS):
            mx = jnp.maximum(mx, s_ref[slot, c * rc:(c + 1) * rc, :])
        m_old = m_ref[hd:hd + 1, :]
        m_new = jnp.maximum(m_old, jnp.max(mx, axis=0, keepdims=True))
        for c in range(SM_CHUNKS):
            rows = slice(c * rc, (c + 1) * rc)
            p_ref[slot, rows, :] = jnp.exp2(s_ref[slot, rows, :] - m_new).astype(BF16)
        al_ref[hd:hd + 1, :] = jnp.exp2(m_old - m_new)
        m_ref[hd:hd + 1, :] = m_new

    def values(off, hd, slot):
        pv = _dot(vt_ref[0, hd, :, pl.ds(off, t_)], p_ref[slot])
        acc_ref[hd] = acc_ref[hd] * al_ref[hd:hd + 1, :] + pv

    def tile_off(j):
        return pl.multiple_of(jnp.clip(j, 0, n_kv - 1) * t_, t_)

    def run_tiles(j0, count):
        items = [(tile_off(j0 + u), hd) for u in range(count) for hd in range(nh)]
        before = (tile_off(j0 - 1), nh - 1)
        after = (tile_off(j0 + count), 0)
        for i, (_, hd) in enumerate(items):
            nxt = items[i + 1] if i + 1 < len(items) else after
            prv = items[i - 1] if i > 0 else before
            scores(nxt[0], nxt[1], nxt[1] % 2)
            values(prv[0], prv[1], prv[1] % 2)
            softmax(hd, hd % 2)

    def group_body(g, carry):
        run_tiles(g * ATTN_UNROLL, ATTN_UNROLL)
        return carry

    def single_body(j, carry):
        run_tiles(j, 1)
        return carry

    scores(0, 0, 0)
    n_grp = n_kv // ATTN_UNROLL
    lax.fori_loop(0, n_grp, group_body, 0)
    lax.fori_loop(n_grp * ATTN_UNROLL, n_kv, single_body, 0)
    values(tile_off(n_kv - 1), nh - 1, (nh - 1) % 2)
    for pr in range(MLA_HEADS // 2):
        ot = jnp.concatenate(
            [acc_ref[hd, 0:MLA_V, :] / acc_ref[hd, MLA_V:MLA_V + 1, :] for hd in (2 * pr, 2 * pr + 1)], axis=0)
        o_ref[0, :, pr * 2 * MLA_V:(pr + 1) * 2 * MLA_V] = ot.T.astype(BF16)


def _attn(q, k, vt, *, n_ctx_tiles, t0):
    b_, s_, _ = q.shape
    t_ = TILE
    nt = s_ // t_
    return pl.pallas_call(
        functools.partial(_attn_kernel, n_ctx_tiles=n_ctx_tiles, n_tiles=nt, t0=t0),
        out_shape=jax.ShapeDtypeStruct((b_, (nt - t0) * t_, MLA_HEADS * MLA_V), BF16),
        grid=(b_, nt - t0),
        in_specs=[
            pl.BlockSpec((1, t_, MLA_HEADS * HEAD_PAD), lambda b, t: (b, t + t0, 0)),
            pl.BlockSpec((1, s_, MLA_HEADS * HEAD_PAD), lambda b, t: (b, 0, 0)),
            pl.BlockSpec((1, MLA_HEADS, V_ROWS, s_), lambda b, t: (b, 0, 0, 0)),
        ],
        out_specs=pl.BlockSpec((1, t_, MLA_HEADS * MLA_V), lambda b, t: (b, t, 0)),
        scratch_shapes=[pltpu.VMEM((MLA_HEADS, t_), F32), pltpu.VMEM((MLA_HEADS, t_), F32),
                        pltpu.VMEM((MLA_HEADS, V_ROWS, t_), F32),
                        pltpu.VMEM((2, t_, t_), F32), pltpu.VMEM((2, t_, t_), BF16)],
        compiler_params=_params(),
        name="attn",
    )(q, k, vt)


def _gla_kernel(qk_ref, g_ref, v_ref, mst_ref, o_ref, st_ref, *, rev):
    t_ = TILE
    ch = GLA_CHUNK

    @pl.when(pl.program_id(1) == 0)
    def _():
        st_ref[...] = jnp.zeros_like(st_ref)

    q = qk_ref[0, :, 0:GLA_KW] * (GLA_DK ** -0.5)
    k = qk_ref[0, :, GLA_KW:2 * GLA_KW]
    v = v_ref[0].astype(BF16)
    cs = _dot_exact_lhs(mst_ref[...], g_ref[0])
    b, bl, mid = cs[0:t_], cs[t_:2 * t_], cs[2 * t_:3 * t_]
    kw = (k * jnp.exp(bl - b)).astype(BF16)
    qe = (q * jnp.exp(b)).astype(BF16)
    q2 = q * jnp.exp(b - mid)
    k2 = (k * jnp.exp(mid - b)).astype(BF16)

    ri = lax.broadcasted_iota(jnp.int32, (t_, t_), 0)
    ci = lax.broadcasted_iota(jnp.int32, (t_, t_), 1)
    same = (ri >> 6) == (ci >> 6)
    tri = (ri <= ci) if rev else (ri >= ci)
    lane_k = lax.broadcasted_iota(jnp.int32, (1, GLA_KW), 1)
    lane_v = lax.broadcasted_iota(jnp.int32, (1, GLA_VW), 1)
    o_intra = jnp.zeros((t_, GLA_VW), F32)
    for hd in range(GLA_HEADS):
        qh = jnp.where((lane_k >> 5) == hd, q2, 0.0).astype(BF16)
        att = _dot_nt(qh, k2)
        att = jnp.where(same, jnp.where(tri, att, 0.0), 0.0).astype(BF16)
        o_intra = o_intra + jnp.where((lane_v >> 6) == hd, _dot(att, v), 0.0)

    rs = lax.broadcasted_iota(jnp.int32, (GLA_VW, GLA_KW), 0)
    cs_ = lax.broadcasted_iota(jnp.int32, (GLA_VW, GLA_KW), 1)
    bd = (rs >> 6) == (cs_ >> 5)
    st = st_ref[...]
    order = range(t_ // ch - 1, -1, -1) if rev else range(t_ // ch)
    for c in order:
        rows = slice(c * ch, (c + 1) * ch)
        o_ref[0, rows, :] = o_intra[rows] + _dot_nt(qe[rows], st.astype(BF16))
        dec = jnp.exp(bl[c * ch:c * ch + 1])
        st = dec * st + jnp.where(bd, _dot_tn(v[rows], kw[rows]), 0.0)
    st_ref[...] = st


def _gla_consts(rev):
    t_, ch = TILE, GLA_CHUNK
    i = np.arange(t_)[:, None]
    j = np.arange(t_)[None, :]
    same = (i // ch) == (j // ch)
    if rev:
        cum = same & (j >= i)
        mid = same & ((j % ch) >= ch - 1 - ch // 2)
    else:
        cum = same & (j <= i)
        mid = same & ((j % ch) <= ch // 2)
    return jnp.asarray(np.concatenate([cum, same, mid], axis=0).astype(np.float32), dtype=BF16)


def _gla(gqk, gg, gv, *, n_ctx_tiles, rev):
    b_, s_, _ = gqk.shape
    t_ = TILE
    nt = s_ // t_
    if rev:
        tile = lambda t: jnp.where(t < n_ctx_tiles, n_ctx_tiles - 1 - t, nt - 1 - (t - n_ctx_tiles))
    else:
        tile = lambda t: t
    d = 1 if rev else 0
    mst = _gla_consts(rev)
    return pl.pallas_call(
        functools.partial(_gla_kernel, rev=rev),
        out_shape=jax.ShapeDtypeStruct((b_, s_, GLA_VW), F32),
        grid=(b_, nt),
        in_specs=[
            pl.BlockSpec((1, t_, 2 * GLA_KW), lambda b, t: (b, tile(t), 0)),
            pl.BlockSpec((1, t_, GLA_KW), lambda b, t: (b, tile(t), d)),
            pl.BlockSpec((1, t_, GLA_VW), lambda b, t: (b, tile(t), 0)),
            _const_spec(mst.shape),
        ],
        out_specs=pl.BlockSpec((1, t_, GLA_VW), lambda b, t: (b, tile(t), 0)),
        scratch_shapes=[pltpu.VMEM((GLA_VW, GLA_KW), F32)],
        compiler_params=_params(),
        name="gla_bwd" if rev else "gla_fwd",
    )(gqk, gg, gv, mst)


def _segment(t, n_ctx_tiles, n_tiles):
    in_ctx = t < n_ctx_tiles
    return jnp.where(in_ctx, t, t - n_ctx_tiles), jnp.where(in_ctx, n_ctx_tiles, n_tiles - n_ctx_tiles)


def _halo_specs(width, n_ctx_tiles, n_tiles, t0):
    per = TILE // HALO

    def prev(b, t):
        tt = t + t0
        ts, _ = _segment(tt, n_ctx_tiles, n_tiles)
        return (b, jnp.where(ts == 0, tt * per, tt * per - 1), 0)

    def nxt(b, t):
        tt = t + t0
        ts, n = _segment(tt, n_ctx_tiles, n_tiles)
        return (b, jnp.where(ts == n - 1, (tt + 1) * per - 1, (tt + 1) * per), 0)

    return pl.BlockSpec((1, HALO, width), prev), pl.BlockSpec((1, HALO, width), nxt)


def _post_kernel(x_ref, mod_ref, pu_ref, pp_ref, pn_ref, att_ref, of_ref, ob_ref, og_ref, gn_ref, wpool_ref,
                 ps_ref, wo_ref, ones_ref, o_ref, ue_ref, *, n_ctx_tiles, n_tiles, t0):
    t_ = TILE
    ts, nseg = _segment(pl.program_id(1) + t0, n_ctx_tiles, n_tiles)
    seg_len = nseg * t_

    ue_ref[0:HALO] = jnp.where(ts > 0, pp_ref[0], 0.0)
    ue_ref[HALO:HALO + t_] = pu_ref[0]
    ue_ref[HALO + t_:2 * HALO + t_] = jnp.where(ts < nseg - 1, pn_ref[0], 0.0)
    pos = ts * t_ + lax.broadcasted_iota(jnp.int32, (t_, 1), 0)
    lane = lax.broadcasted_iota(jnp.int32, (1, LANES), 1)

    def shifted(offsets, c0):
        acc = None
        for o in offsets:
            r = ue_ref[pl.ds(HALO + o, t_), c0:c0 + LANES]
            acc = r if acc is None else acc + r
        return acc

    def count(w):
        lo = pos - w // 2
        return (jnp.minimum(lo + w, seg_len) - jnp.maximum(lo, 0)).astype(F32)

    w2 = shifted((-1, 0), 0)
    w4 = w2 + shifted((-2, 1), 0)
    w8 = shifted(range(-4, 4), LANES)
    w16 = w8 + shifted(tuple(range(-8, -4)) + tuple(range(4, 8)), LANES)
    u = pu_ref[0]
    d01 = jnp.where(lane < POOL_GC, w2 / count(2), w4 / count(4)) - u[:, 0:LANES]
    d23 = jnp.where(lane < POOL_GC, w8 / count(8), w16 / count(16)) - u[:, LANES:2 * LANES]
    diff = jnp.concatenate([d01, d23], axis=1).astype(BF16)
    ypool = _dot(diff, wpool_ref[...]) * ps_ref[...]

    o = of_ref[0] + ob_ref[0]
    ms = _dot_exact_rhs(o * o, ones_ref[...]) * (1.0 / GLA_DV)
    yg = o * lax.rsqrt(ms + EPS) * gn_ref[...] * _silu(og_ref[0])

    ycat = jnp.concatenate([ypool.astype(BF16), att_ref[0], yg.astype(BF16)], axis=1)
    o_ref[0] = x_ref[0] + mod_ref[0][2:3] * _dot(ycat, wo_ref[...])


def _post(xa, mod, pu, att, of, ob, og, gn, wpool, ps, wo, ones, *, n_ctx_tiles, ctx_row, t0):
    b_, s_, d = xa.shape
    t_ = TILE
    nt = s_ // t_
    tok = lambda w: pl.BlockSpec((1, t_, w), lambda b, t: (b, t + t0, 0))
    hp, hn = _halo_specs(POOL_W, n_ctx_tiles, nt, t0)
    return pl.pallas_call(
        functools.partial(_post_kernel, n_ctx_tiles=n_ctx_tiles, n_tiles=nt, t0=t0),
        out_shape=jax.ShapeDtypeStruct((b_, (nt - t0) * t_, d), F32),
        grid=(b_, nt - t0),
        in_specs=[
            tok(d),
            pl.BlockSpec((1, 6, d), lambda b, t: (jnp.where(t + t0 < n_ctx_tiles, ctx_row, b), 0, 0)),
            tok(POOL_W), hp, hn,
            pl.BlockSpec((1, t_, MLA_HEADS * MLA_V), lambda b, t: (b, t, 0)),
            tok(GLA_VW), tok(GLA_VW), tok(GLA_VW),
            _const_spec(gn.shape), _const_spec(wpool.shape), _const_spec(ps.shape), _const_spec(wo.shape),
            _const_spec(ones.shape),
        ],
        out_specs=pl.BlockSpec((1, t_, d), lambda b, t: (b, t, 0)),
        scratch_shapes=[pltpu.VMEM((t_ + 2 * HALO, POOL_W), F32)],
        compiler_params=_params(),
        name="post",
    )(xa, mod, pu, pu, pu, att, of, ob, og, gn, wpool, ps, wo, ones)


def _ffn_kernel(x_ref, xp_ref, xn_ref, mod_ref, n2_ref, wup_ref, cw_ref, cb_ref, wdn_ref, nf_ref, o_ref, ge_ref,
                *, n_ctx_tiles, n_tiles, final):
    t_ = TILE
    fc = D_FF // FF_CHUNKS
    ts, nseg = _segment(pl.program_id(1), n_ctx_tiles, n_tiles)
    mod = mod_ref[0]
    n2 = n2_ref[...]

    def hmod(xx):
        return (_rms(xx, n2) * (1.0 + mod[4:5]) + mod[3:4]).astype(BF16)

    x = x_ref[0]
    h = hmod(x)
    hp = hmod(xp_ref[0])
    hn = hmod(xn_ref[0])
    y = jnp.zeros_like(x)
    for c in range(FF_CHUNKS):
        cu = slice(c * fc, (c + 1) * fc)
        cg = slice(D_FF + c * fc, D_FF + (c + 1) * fc)
        u = _dot(h, wup_ref[:, cu])
        g = _dot(h, wup_ref[:, cg])
        ge_ref[0:HALO] = jnp.where(ts > 0, _dot(hp, wup_ref[:, cg]), 0.0)
        ge_ref[HALO:HALO + t_] = g
        ge_ref[HALO + t_:2 * HALO + t_] = jnp.where(ts < nseg - 1, _dot(hn, wup_ref[:, cg]), 0.0)
        cw = cw_ref[:, cu]
        gc = (ge_ref[pl.ds(HALO - 1, t_), :] * cw[0:1] + g * cw[1:2]
              + ge_ref[pl.ds(HALO + 1, t_), :] * cw[2:3] + cb_ref[:, cu])
        y = y + _dot((_silu(gc) * u).astype(BF16), wdn_ref[cu, :])
    x2 = x + mod[5:6] * y
    o_ref[0] = _rms(x2, nf_ref[...]) if final else x2


def _ffn(x1, mod, n2, wup, cw, cb, wdn, nf, *, n_ctx_tiles, ctx_row, final):
    b_, s_, d = x1.shape
    t_ = TILE
    nt = s_ // t_
    hp, hn = _halo_specs(d, n_ctx_tiles, nt, 0)
    return pl.pallas_call(
        functools.partial(_ffn_kernel, n_ctx_tiles=n_ctx_tiles, n_tiles=nt, final=final),
        out_shape=jax.ShapeDtypeStruct((b_, s_, d), F32),
        grid=(b_, nt),
        in_specs=[
            pl.BlockSpec((1, t_, d), lambda b, t: (b, t, 0)), hp, hn,
            pl.BlockSpec((1, 6, d), lambda b, t: (jnp.where(t < n_ctx_tiles, ctx_row, b), 0, 0)),
            _const_spec(n2.shape), _const_spec(wup.shape), _const_spec(cw.shape), _const_spec(cb.shape),
            _const_spec(wdn.shape), _const_spec(nf.shape),
        ],
        out_specs=pl.BlockSpec((1, t_, d), lambda b, t: (b, t, 0)),
        scratch_shapes=[pltpu.VMEM((t_ + 2 * HALO, D_FF // FF_CHUNKS), F32)],
        compiler_params=_params(),
        name="ffn",
    )(x1, x1, x1, mod, n2, wup, cw, cb, wdn, nf)


def _rot_cols(w):
    w4 = w.reshape(w.shape[:-1] + (2, 2, MLA_ROPE // 4))
    return jnp.stack([-w4[..., 1, :], w4[..., 0, :]], axis=-2).reshape(w.shape)


def _prep_weights(w_in, w_uq, w_ukv, w_gk_f, b_gk_f, w_gk_b, b_gk_b, w_pool, gla_norm):
    depth, d, _ = w_in.shape
    z = lambda *s: jnp.zeros((depth,) + s, F32)
    o = 0
    parts = {}
    for name, size in (("ckv", KV_LORA), ("kr", MLA_ROPE), ("gk", GLA_KW), ("gv", GLA_VW),
                       ("lrf", GLA_GATE_RANK), ("lrb", GLA_GATE_RANK), ("pool", POOL_W), ("cq", Q_LORA),
                       ("gq", GLA_KW), ("og", GLA_VW)):
        parts[name] = w_in[..., o:o + size]
        o += size
    win = jnp.concatenate([
        parts["ckv"], parts["cq"], parts["pool"], parts["gq"], parts["gk"], parts["gv"], parts["og"],
        parts["lrf"], parts["lrb"], z(d, 32), parts["kr"], z(d, 32),
        z(d, 64), _rot_cols(parts["kr"]), z(d, 32)], axis=-1).astype(BF16)

    uq = w_uq.reshape(depth, Q_LORA, MLA_HEADS, MLA_QK)
    nope, rope = uq[..., :MLA_NOPE], uq[..., MLA_NOPE:]
    zq = lambda w: jnp.zeros((depth, Q_LORA, MLA_HEADS, w), F32)
    qa = jnp.concatenate([nope, rope, zq(32)], axis=-1).reshape(depth, Q_LORA, -1)
    qb = jnp.concatenate([zq(64), _rot_cols(rope), zq(32)], axis=-1).reshape(depth, Q_LORA, -1)
    wuq = jnp.concatenate([qa, qb], axis=-1).astype(BF16)

    ukv = w_ukv.reshape(depth, KV_LORA, MLA_HEADS, MLA_NOPE + MLA_V)
    kn = jnp.concatenate([ukv[..., :MLA_NOPE], jnp.zeros((depth, KV_LORA, MLA_HEADS, 64), F32)], axis=-1)
    wuk = kn.reshape(depth, KV_LORA, -1).astype(BF16)
    wvt = jnp.swapaxes(ukv[..., MLA_NOPE:].reshape(depth, KV_LORA, -1), 1, 2).astype(BF16)

    r = GLA_GATE_RANK
    wg = jnp.zeros((depth, LANES, 2 * GLA_KW), F32)
    wg = wg.at[:, 0:r, 0:GLA_KW].set(w_gk_f).at[:, r:2 * r, GLA_KW:].set(w_gk_b).astype(BF16)
    bg = jnp.concatenate([b_gk_f, b_gk_b], axis=-1)[:, None, :]

    wpool = jnp.zeros((depth, POOL_W, POOL_W), F32)
    for g in range(POOL_GROUPS):
        sl = slice(g * POOL_GC, (g + 1) * POOL_GC)
        wpool = wpool.at[:, sl, sl].set(w_pool[:, g])
    gn = jnp.tile(gla_norm, (1, GLA_HEADS))[:, None, :]
    return win, wuq, wuk, wvt, wg, bg, wpool.astype(BF16), gn


def _rope_tables(seq, ctx_len):
    rows = seq // GRID_W
    row = jnp.repeat(jnp.arange(rows), GRID_W).astype(F32)
    col = jnp.tile(jnp.arange(GRID_W), rows).astype(F32)
    half = MLA_ROPE // 2
    inv = ROPE_THETA ** (-jnp.arange(0, half, 2, dtype=F32) / half)
    ar = row[:, None] * inv
    ac = col[:, None] * inv
    ang = jnp.concatenate([ar, ar, ac, ac], axis=-1)
    cos = jnp.concatenate([jnp.ones((ctx_len, MLA_ROPE), F32), jnp.cos(ang)], axis=0)
    sin = jnp.concatenate([jnp.zeros((ctx_len, MLA_ROPE), F32), jnp.sin(ang)], axis=0)
    s_ = ctx_len + seq
    scale = MLA_QK ** -0.5 * LOG2E
    z = lambda w: jnp.zeros((s_, w), F32)
    return jnp.concatenate([
        jnp.ones((s_, MLA_NOPE), F32) * scale, cos * scale, z(32),
        z(MLA_NOPE), sin * scale, z(32),
        z(MLA_NOPE), cos, z(32),
        z(MLA_NOPE), sin, z(32)], axis=-1)


def kernel(x, c, ctx, c_ctx, w_ada, b_ada, norm1, norm2, w_in, q_norm, w_uq, kv_norm, w_ukv, w_gk_f, b_gk_f,
           w_gk_b, b_gk_b, gla_norm, w_pool, pool_scale, w_o, w_up, conv_w, conv_b, w_down, norm_f):
    b_, seq, d = x.shape
    ctx_len = ctx.shape[1]
    depth = w_ada.shape[0]
    assert seq % TILE == 0 and ctx_len % TILE == 0 and seq % GRID_W == 0
    nc = ctx_len // TILE

    bp = -(-(b_ + 1) // 8) * 8
    cvec = jnp.concatenate([c, c_ctx[None, :], jnp.zeros((bp - b_ - 1, d), F32)], axis=0)
    mod_all = _ada(cvec, w_ada, b_ada).reshape(depth, bp, 6, d)

    win, wuq, wuk, wvt, wg, bg, wpool, gn = _prep_weights(w_in, w_uq, w_ukv, w_gk_f, b_gk_f, w_gk_b, b_gk_b,
                                                      w_pool, gla_norm)
    wo = w_o.astype(BF16)
    wup = w_up.astype(BF16)
    wdn = w_down.astype(BF16)
    tab = _rope_tables(seq, ctx_len)
    gi = np.arange(GLA_VW) // GLA_DV
    ones = jnp.asarray((gi[:, None] == gi[None, :]).astype(np.float32), dtype=BF16)

    xa = jnp.concatenate([ctx, x], axis=1)
    for i in range(depth):
        last = i == depth - 1
        t0 = nc if last else 0
        mod = mod_all[i]
        q, k, v, pu, gqk, gg, gv, og = _pre(xa, mod, norm1[i][None], win[i], q_norm[i][None], wuq[i],
                                            kv_norm[i][None], wuk[i], wvt[i], wg[i], bg[i], tab,
                                            n_ctx_tiles=nc, ctx_row=b_)
        att = _attn(q, k, v, n_ctx_tiles=nc, t0=t0)
        of = _gla(gqk, gg, gv, n_ctx_tiles=nc, rev=False)
        ob = _gla(gqk, gg, gv, n_ctx_tiles=nc, rev=True)
        x1 = _post(xa, mod, pu, att, of, ob, og, gn[i], wpool[i], pool_scale[i][None], wo[i], ones,
                   n_ctx_tiles=nc, ctx_row=b_, t0=t0)
        xa = _ffn(x1, mod, norm2[i][None], wup[i], conv_w[i], conv_b[i][None], wdn[i], norm_f[None],
                  n_ctx_tiles=0 if last else nc, ctx_row=b_, final=last)
    return xa
```

```python
import functools

import numpy as np
import jax
import jax.numpy as jnp
from jax import lax
from jax.experimental import pallas as pl
from jax.experimental.pallas import tpu as pltpu

F32 = jnp.float32
BF16 = jnp.bfloat16

EPS = 1e-6
LOG2E = 1.4426950408889634
GRID_W = 64
POOL_W = 256
POOL_GROUPS = 4
POOL_GC = POOL_W // POOL_GROUPS
POOL_WINDOWS = (2, 4, 8, 16)
MLA_HEADS = 8
MLA_NOPE = 64
MLA_ROPE = 32
MLA_V = 64
MLA_QK = MLA_NOPE + MLA_ROPE
Q_LORA = 256
KV_LORA = 128
ROPE_THETA = 10000.0
GLA_HEADS = 4
GLA_DK = 32
GLA_DV = 64
GLA_GATE_RANK = 16
GLA_GATE_NORM = 16.0
GLA_CHUNK = 64
GLA_KW = GLA_HEADS * GLA_DK
GLA_VW = GLA_HEADS * GLA_DV
D_FF = 2816

TILE = 256
HALO = 8
LANES = 128
HEAD_PAD = LANES
V_ROWS = 80
MIX_W = POOL_W + MLA_HEADS * MLA_V + GLA_VW
FF_CHUNKS = 2
SM_CHUNKS = 4
ATTN_UNROLL = 4
ATTN_SLOTS = 8
VMEM_LIMIT = 56 * 1024 * 1024

C_CKV = 0
C_CQ = C_CKV + KV_LORA
C_POOL = C_CQ + Q_LORA
C_GQK = C_POOL + POOL_W
C_GV = C_GQK + 2 * GLA_KW
C_OG = C_GV + GLA_VW
C_MISC_A = C_OG + GLA_VW
C_MISC_B = C_MISC_A + LANES
IN_COLS_P = C_MISC_B + LANES


def _dot(a, b):
    return jnp.dot(a, b, preferred_element_type=F32)


def _dot_nt(a, b):
    return lax.dot_general(a, b, (((1,), (1,)), ((), ())), preferred_element_type=F32)


def _dot_tn(a, b):
    return lax.dot_general(a, b, (((0,), (0,)), ((), ())), preferred_element_type=F32)


def _dot_exact_rhs(a, m):
    a1 = a.astype(BF16)
    r = a - a1.astype(F32)
    a2 = r.astype(BF16)
    a3 = (r - a2.astype(F32)).astype(BF16)
    return _dot(a1, m) + _dot(a2, m) + _dot(a3, m)


def _dot_exact_lhs(m, a):
    a1 = a.astype(BF16)
    r = a - a1.astype(F32)
    a2 = r.astype(BF16)
    a3 = (r - a2.astype(F32)).astype(BF16)
    return _dot(m, a1) + _dot(m, a2) + _dot(m, a3)


def _rms(x, g):
    return x * lax.rsqrt(jnp.mean(x * x, axis=-1, keepdims=True) + EPS) * g


def _silu(x):
    return x * jax.nn.sigmoid(x)


def _params():
    return pltpu.CompilerParams(vmem_limit_bytes=VMEM_LIMIT)


def _const_spec(shape):
    nd = len(shape)
    return pl.BlockSpec(shape, lambda *_: (0,) * nd, pipeline_mode=pl.Buffered(1))


def _ada_kernel(c_ref, w_ref, b_ref, o_ref):
    s = _silu(c_ref[...]).astype(BF16)
    o_ref[0] = _dot(s, w_ref[0].astype(BF16)) + b_ref[0]


def _ada(cvec, w_ada, b_ada):
    depth, d, n = w_ada.shape
    bp = cvec.shape[0]
    tn = n // 4
    return pl.pallas_call(
        _ada_kernel,
        out_shape=jax.ShapeDtypeStruct((depth, bp, n), F32),
        grid=(depth, n // tn),
        in_specs=[
            pl.BlockSpec((bp, d), lambda i, j: (0, 0)),
            pl.BlockSpec((1, d, tn), lambda i, j: (i, 0, j)),
            pl.BlockSpec((1, 1, tn), lambda i, j: (i, 0, j)),
        ],
        out_specs=pl.BlockSpec((1, bp, tn), lambda i, j: (i, 0, j)),
        compiler_params=_params(),
        name="ada",
    )(cvec, w_ada, b_ada.reshape(depth, 1, n))


def _pre_kernel(x_ref, mod_ref, n1_ref, win_ref, qn_ref, wuqt_ref, kvn_ref, wuk_ref, wvt_ref, wg_ref, bg_ref, tabk_ref,
                tabq_ref, qt_ref, k_ref, vt_ref, pu_ref, gqk_ref, gg_ref, gv_ref, og_ref):
    x = x_ref[0]
    mod = mod_ref[0]
    h = _rms(x, n1_ref[...]) * (1.0 + mod[1:2]) + mod[0:1]
    z = _dot(h.astype(BF16), win_ref[...])
    pu_ref[0] = z[:, C_POOL:C_POOL + POOL_W]
    gqk_ref[0] = z[:, C_GQK:C_GQK + 2 * GLA_KW]
    gv_ref[0] = z[:, C_GV:C_GV + GLA_VW]
    og_ref[0] = z[:, C_OG:C_OG + GLA_VW]
    za = z[:, C_MISC_A:C_MISC_A + LANES]
    zb = z[:, C_MISC_B:C_MISC_B + LANES]
    cosk, sink = tabk_ref[:, 0:LANES], tabk_ref[:, LANES:2 * LANES]

    a = _dot(za.astype(BF16), wg_ref[...]) + bg_ref[...]
    gg_ref[0] = (jnp.minimum(a, 0.0) - jnp.log1p(jnp.exp(-jnp.abs(a)))) * (1.0 / GLA_GATE_NORM)

    ckvn = _rms(z[:, C_CKV:C_CKV + KV_LORA], kvn_ref[...]).astype(BF16)
    uk = _dot(ckvn, wuk_ref[...])
    kr = za * cosk + zb * sink
    kw = MLA_HEADS * HEAD_PAD
    for hd in range(MLA_HEADS):
        sl = slice(hd * HEAD_PAD, (hd + 1) * HEAD_PAD)
        k_ref[0, :, sl] = (uk[:, sl] + kr).astype(BF16)
    vt = _dot_nt(wvt_ref[...], ckvn)
    pad_row = lax.broadcasted_iota(jnp.int32, (V_ROWS - MLA_V, x.shape[0]), 0)
    pad = jnp.where(pad_row == 0, 1.0, 0.0).astype(BF16)
    for hd in range(MLA_HEADS):
        vt_ref[0, hd, 0:MLA_V, :] = vt[hd * MLA_V:(hd + 1) * MLA_V].astype(BF16)
        vt_ref[0, hd, MLA_V:V_ROWS, :] = pad

    cqn = _rms(z[:, C_CQ:C_CQ + Q_LORA], qn_ref[...]).astype(BF16)
    uqt = _dot_nt(wuqt_ref[...], cqn)
    cosq, sinq = tabq_ref[0:LANES, :], tabq_ref[LANES:2 * LANES, :]
    for hd in range(MLA_HEADS):
        sl = slice(hd * HEAD_PAD, (hd + 1) * HEAD_PAD)
        sr = slice(kw + hd * HEAD_PAD, kw + (hd + 1) * HEAD_PAD)
        qt_ref[0, sl, :] = (uqt[sl] * cosq + uqt[sr] * sinq).astype(BF16)


def _pre(xa, mod, n1, win, qn, wuqt, kvn, wuk, wvt, wg, bg, tabk, tabq, *, n_ctx_tiles, ctx_row):
    b_, s_, d = xa.shape
    t_ = TILE
    nt = s_ // t_
    tok = lambda w: pl.BlockSpec((1, t_, w), lambda b, t: (b, t, 0))
    outs = [(MLA_HEADS * HEAD_PAD, BF16), (POOL_W, F32), (2 * GLA_KW, F32), (2 * GLA_KW, F32), (GLA_VW, F32),
            (GLA_VW, F32)]
    qt_shape = jax.ShapeDtypeStruct((b_, MLA_HEADS * HEAD_PAD, s_), BF16)
    qt_spec = pl.BlockSpec((1, MLA_HEADS * HEAD_PAD, t_), lambda b, t: (b, 0, t))
    vt_shape = jax.ShapeDtypeStruct((b_, MLA_HEADS, V_ROWS, s_), BF16)
    vt_spec = pl.BlockSpec((1, MLA_HEADS, V_ROWS, t_), lambda b, t: (b, 0, 0, t))
    tok_shapes = [jax.ShapeDtypeStruct((b_, s_, w), dt) for w, dt in outs]
    tok_specs = [tok(w) for w, _ in outs]
    return pl.pallas_call(
        _pre_kernel,
        out_shape=[qt_shape, tok_shapes[0], vt_shape] + tok_shapes[1:],
        grid=(b_, nt),
        in_specs=[
            tok(d),
            pl.BlockSpec((1, 6, d), lambda b, t: (jnp.where(t < n_ctx_tiles, ctx_row, b), 0, 0)),
            _const_spec(n1.shape), _const_spec(win.shape), _const_spec(qn.shape), _const_spec(wuqt.shape),
            _const_spec(kvn.shape), _const_spec(wuk.shape), _const_spec(wvt.shape), _const_spec(wg.shape),
            _const_spec(bg.shape),
            pl.BlockSpec((t_, 2 * LANES), lambda b, t: (t, 0)),
            pl.BlockSpec((2 * LANES, t_), lambda b, t: (0, t)),
        ],
        out_specs=[qt_spec, tok_specs[0], vt_spec] + tok_specs[1:],
        compiler_params=_params(),
        name="pre",
    )(xa, mod, n1, win, qn, wuqt, kvn, wuk, wvt, wg, bg, tabk, tabq)


def _attn_kernel(qt_ref, k_ref, vt_ref, o_ref, m_ref, al_ref, acc_ref, s_ref, p_ref, *, n_ctx_tiles, n_tiles, t0):
    t_ = TILE
    nh = MLA_HEADS
    rc = t_ // SM_CHUNKS
    qi = pl.program_id(1) + t0
    n_kv = jnp.where(qi < n_ctx_tiles, n_ctx_tiles, n_tiles)
    m_ref[...] = jnp.full(m_ref.shape, -jnp.inf, F32)
    al_ref[...] = jnp.ones(al_ref.shape, F32)
    acc_ref[...] = jnp.zeros(acc_ref.shape, F32)
    p_ref[...] = jnp.zeros(p_ref.shape, BF16)

    def scores(off, hd, slot):
        sl = slice(hd * HEAD_PAD, (hd + 1) * HEAD_PAD)
        s_ref[slot] = _dot(k_ref[0, pl.ds(off, t_), sl], qt_ref[0, sl, :])

    def softmax(hd, slot):
        mx = s_ref[slot, 0:rc, :]
        for c in range(1, SM_CHUNKS):
            mx = jnp.maximum(mx, s_ref[slot, c * rc:(c + 1) * rc, :])
        m_old = m_ref[hd:hd + 1, :]
        m_new = jnp.maximum(m_old, jnp.max(mx, axis=0, keepdims=True))
        for c in range(SM_CHUNKS):
            rows = slice(c * rc, (c + 1) * rc)
            p_ref[slot, rows, :] = jnp.exp2(s_ref[slot, rows, :] - m_new).astype(BF16)
        al_ref[hd:hd + 1, :] = jnp.exp2(m_old - m_new)
        m_ref[hd:hd + 1, :] = m_new

    def values(off, hd, slot):
        pv = _dot(vt_ref[0, hd, :, pl.ds(off, t_)], p_ref[slot])
        acc_ref[hd] = acc_ref[hd] * al_ref[hd:hd + 1, :] + pv

    def tile_off(j):
        return pl.multiple_of(jnp.clip(j, 0, n_kv - 1) * t_, t_)

    def run_tiles(j0, count):
        items = [(tile_off(j0 + u), hd) for u in range(count) for hd in range(nh)]
        before = (tile_off(j0 - 1), nh - 1)
        after = (tile_off(j0 + count), 0)
        for i, (_, hd) in enumerate(items):
            nxt = items[i + 1] if i + 1 < len(items) else after
            prv = items[i - 1] if i > 0 else before
            scores(nxt[0], nxt[1], nxt[1] % ATTN_SLOTS)
            values(prv[0], prv[1], prv[1] % ATTN_SLOTS)
            softmax(hd, hd % ATTN_SLOTS)

    def group_body(g, carry):
        run_tiles(g * ATTN_UNROLL, ATTN_UNROLL)
        return carry

    def single_body(j, carry):
        run_tiles(j, 1)
        return carry

    scores(0, 0, 0)
    n_grp = n_kv // ATTN_UNROLL
    lax.fori_loop(0, n_grp, group_body, 0)
    lax.fori_loop(n_grp * ATTN_UNROLL, n_kv, single_body, 0)
    values(tile_off(n_kv - 1), nh - 1, (nh - 1) % ATTN_SLOTS)
    for pr in range(MLA_HEADS // 2):
        ot = jnp.concatenate(
            [acc_ref[hd, 0:MLA_V, :] / acc_ref[hd, MLA_V:MLA_V + 1, :] for hd in (2 * pr, 2 * pr + 1)], axis=0)
        o_ref[0, :, pr * 2 * MLA_V:(pr + 1) * 2 * MLA_V] = ot.T.astype(BF16)


def _attn(qt, k, vt, *, n_ctx_tiles, t0):
    b_, s_, _ = k.shape
    t_ = TILE
    nt = s_ // t_
    return pl.pallas_call(
        functools.partial(_attn_kernel, n_ctx_tiles=n_ctx_tiles, n_tiles=nt, t0=t0),
        out_shape=jax.ShapeDtypeStruct((b_, (nt - t0) * t_, MLA_HEADS * MLA_V), BF16),
        grid=(b_, nt - t0),
        in_specs=[
            pl.BlockSpec((1, MLA_HEADS * HEAD_PAD, t_), lambda b, t: (b, 0, t + t0)),
            pl.BlockSpec((1, s_, MLA_HEADS * HEAD_PAD), lambda b, t: (b, 0, 0)),
            pl.BlockSpec((1, MLA_HEADS, V_ROWS, s_), lambda b, t: (b, 0, 0, 0)),
        ],
        out_specs=pl.BlockSpec((1, t_, MLA_HEADS * MLA_V), lambda b, t: (b, t, 0)),
        scratch_shapes=[pltpu.VMEM((MLA_HEADS, t_), F32), pltpu.VMEM((MLA_HEADS, t_), F32),
                        pltpu.VMEM((MLA_HEADS, V_ROWS, t_), F32),
                        pltpu.VMEM((ATTN_SLOTS, t_, t_), F32), pltpu.VMEM((ATTN_SLOTS, t_, t_), BF16)],
        compiler_params=_params(),
        name="attn",
    )(qt, k, vt)


def _gla_kernel(qkf_ref, gf_ref, vf_ref, qkb_ref, gb_ref, vb_ref, mstf_ref, mstb_ref, of_ref, ob_ref, stf_ref,
                stb_ref):
    @pl.when(pl.program_id(1) == 0)
    def _():
        stf_ref[...] = jnp.zeros_like(stf_ref)
        stb_ref[...] = jnp.zeros_like(stb_ref)

    _gla_direction(qkf_ref, gf_ref, vf_ref, mstf_ref, of_ref, stf_ref, rev=False)
    _gla_direction(qkb_ref, gb_ref, vb_ref, mstb_ref, ob_ref, stb_ref, rev=True)


def _gla_direction(qk_ref, g_ref, v_ref, mst_ref, o_ref, st_ref, *, rev):
    t_ = TILE
    ch = GLA_CHUNK

    q = qk_ref[0, :, 0:GLA_KW] * (GLA_DK ** -0.5)
    k = qk_ref[0, :, GLA_KW:2 * GLA_KW]
    v = v_ref[0].astype(BF16)
    b = _dot_exact_lhs(mst_ref[...], g_ref[0])
    end_row = 0 if rev else ch - 1
    mid_row = ch - 1 - ch // 2 if rev else ch // 2

    def chunk_rows(r):
        return jnp.concatenate(
            [jnp.broadcast_to(b[c * ch + r:c * ch + r + 1], (ch, GLA_KW)) for c in range(t_ // ch)], axis=0)

    bl, mid = chunk_rows(end_row), chunk_rows(mid_row)
    kw = (k * jnp.exp(bl - b)).astype(BF16)
    qe = (q * jnp.exp(b)).astype(BF16)
    q2 = q * jnp.exp(b - mid)
    k2 = (k * jnp.exp(mid - b)).astype(BF16)

    ri = lax.broadcasted_iota(jnp.int32, (t_, t_), 0)
    ci = lax.broadcasted_iota(jnp.int32, (t_, t_), 1)
    same = (ri >> 6) == (ci >> 6)
    tri = (ri <= ci) if rev else (ri >= ci)
    lane_k = lax.broadcasted_iota(jnp.int32, (1, GLA_KW), 1)
    lane_v = lax.broadcasted_iota(jnp.int32, (1, GLA_VW), 1)
    o_intra = jnp.zeros((t_, GLA_VW), F32)
    for hd in range(GLA_HEADS):
        qh = jnp.where((lane_k >> 5) == hd, q2, 0.0).astype(BF16)
        att = _dot_nt(qh, k2)
        att = jnp.where(same, jnp.where(tri, att, 0.0), 0.0).astype(BF16)
        o_intra = o_intra + jnp.where((lane_v >> 6) == hd, _dot(att, v), 0.0)

    rs = lax.broadcasted_iota(jnp.int32, (GLA_VW, GLA_KW), 0)
    cs_ = lax.broadcasted_iota(jnp.int32, (GLA_VW, GLA_KW), 1)
    bd = (rs >> 6) == (cs_ >> 5)
    st = st_ref[...]
    order = range(t_ // ch - 1, -1, -1) if rev else range(t_ // ch)
    for c in order:
        rows = slice(c * ch, (c + 1) * ch)
        o_ref[0, rows, :] = o_intra[rows] + _dot_nt(qe[rows], st.astype(BF16))
        dec = jnp.exp(bl[c * ch:c * ch + 1])
        st = dec * st + jnp.where(bd, _dot_tn(v[rows], kw[rows]), 0.0)
    st_ref[...] = st


def _gla_consts(rev):
    t_, ch = TILE, GLA_CHUNK
    i = np.arange(t_)[:, None]
    j = np.arange(t_)[None, :]
    same = (i // ch) == (j // ch)
    cum = same & ((j >= i) if rev else (j <= i))
    return jnp.asarray(cum.astype(np.float32), dtype=BF16)


def _gla(gqk, gg, gv, *, n_ctx_tiles):
    b_, s_, _ = gqk.shape
    t_ = TILE
    nt = s_ // t_
    fwd = lambda t: t
    bwd = lambda t: jnp.where(t < n_ctx_tiles, n_ctx_tiles - 1 - t, nt - 1 - (t - n_ctx_tiles))
    mstf, mstb = _gla_consts(False), _gla_consts(True)

    def specs(tile, d):
        return [pl.BlockSpec((1, t_, 2 * GLA_KW), lambda b, t: (b, tile(t), 0)),
                pl.BlockSpec((1, t_, GLA_KW), lambda b, t: (b, tile(t), d)),
                pl.BlockSpec((1, t_, GLA_VW), lambda b, t: (b, tile(t), 0))]

    out = jax.ShapeDtypeStruct((b_, s_, GLA_VW), F32)
    return pl.pallas_call(
        _gla_kernel,
        out_shape=[out, out],
        grid=(b_, nt),
        in_specs=specs(fwd, 0) + specs(bwd, 1) + [_const_spec(mstf.shape), _const_spec(mstb.shape)],
        out_specs=[pl.BlockSpec((1, t_, GLA_VW), lambda b, t: (b, fwd(t), 0)),
                   pl.BlockSpec((1, t_, GLA_VW), lambda b, t: (b, bwd(t), 0))],
        scratch_shapes=[pltpu.VMEM((GLA_VW, GLA_KW), F32), pltpu.VMEM((GLA_VW, GLA_KW), F32)],
        compiler_params=_params(),
        name="gla",
    )(gqk, gg, gv, gqk, gg, gv, mstf, mstb)


def _segment(t, n_ctx_tiles, n_tiles):
    in_ctx = t < n_ctx_tiles
    return jnp.where(in_ctx, t, t - n_ctx_tiles), jnp.where(in_ctx, n_ctx_tiles, n_tiles - n_ctx_tiles)


def _halo_specs(width, n_ctx_tiles, n_tiles, t0):
    per = TILE // HALO

    def prev(b, t):
        tt = t + t0
        ts, _ = _segment(tt, n_ctx_tiles, n_tiles)
        return (b, jnp.where(ts == 0, tt * per, tt * per - 1), 0)

    def nxt(b, t):
        tt = t + t0
        ts, n = _segment(tt, n_ctx_tiles, n_tiles)
        return (b, jnp.where(ts == n - 1, (tt + 1) * per - 1, (tt + 1) * per), 0)

    return pl.BlockSpec((1, HALO, width), prev), pl.BlockSpec((1, HALO, width), nxt)


def _post_kernel(x_ref, mod_ref, pu_ref, pp_ref, pn_ref, att_ref, of_ref, ob_ref, og_ref, gn_ref, wpool_ref,
                 ps_ref, wo_ref, ones_ref, o_ref, ue_ref, *, n_ctx_tiles, n_tiles, t0):
    t_ = TILE
    ts, nseg = _segment(pl.program_id(1) + t0, n_ctx_tiles, n_tiles)
    seg_len = nseg * t_

    ue_ref[0:HALO] = jnp.where(ts > 0, pp_ref[0], 0.0)
    ue_ref[HALO:HALO + t_] = pu_ref[0]
    ue_ref[HALO + t_:2 * HALO + t_] = jnp.where(ts < nseg - 1, pn_ref[0], 0.0)
    pos = ts * t_ + lax.broadcasted_iota(jnp.int32, (t_, 1), 0)
    lane = lax.broadcasted_iota(jnp.int32, (1, LANES), 1)

    def shifted(offsets, c0):
        acc = None
        for o in offsets:
            r = ue_ref[pl.ds(HALO + o, t_), c0:c0 + LANES]
            acc = r if acc is None else acc + r
        return acc

    def count(w):
        lo = pos - w // 2
        return (jnp.minimum(lo + w, seg_len) - jnp.maximum(lo, 0)).astype(F32)

    w2 = shifted((-1, 0), 0)
    w4 = w2 + shifted((-2, 1), 0)
    w8 = shifted(range(-4, 4), LANES)
    w16 = w8 + shifted(tuple(range(-8, -4)) + tuple(range(4, 8)), LANES)
    u = pu_ref[0]
    d01 = jnp.where(lane < POOL_GC, w2 / count(2), w4 / count(4)) - u[:, 0:LANES]
    d23 = jnp.where(lane < POOL_GC, w8 / count(8), w16 / count(16)) - u[:, LANES:2 * LANES]
    diff = jnp.concatenate([d01, d23], axis=1).astype(BF16)
    ypool = _dot(diff, wpool_ref[...]) * ps_ref[...]

    o = of_ref[0] + ob_ref[0]
    ms = _dot_exact_rhs(o * o, ones_ref[...]) * (1.0 / GLA_DV)
    yg = o * lax.rsqrt(ms + EPS) * gn_ref[...] * _silu(og_ref[0])

    ycat = jnp.concatenate([ypool.astype(BF16), att_ref[0], yg.astype(BF16)], axis=1)
    o_ref[0] = x_ref[0] + mod_ref[0][2:3] * _dot(ycat, wo_ref[...])


def _post(xa, mod, pu, att, of, ob, og, gn, wpool, ps, wo, ones, *, n_ctx_tiles, ctx_row, t0):
    b_, s_, d = xa.shape
    t_ = TILE
    nt = s_ // t_
    tok = lambda w: pl.BlockSpec((1, t_, w), lambda b, t: (b, t + t0, 0))
    hp, hn = _halo_specs(POOL_W, n_ctx_tiles, nt, t0)
    return pl.pallas_call(
        functools.partial(_post_kernel, n_ctx_tiles=n_ctx_tiles, n_tiles=nt, t0=t0),
        out_shape=jax.ShapeDtypeStruct((b_, (nt - t0) * t_, d), F32),
        grid=(b_, nt - t0),
        in_specs=[
            tok(d),
            pl.BlockSpec((1, 6, d), lambda b, t: (jnp.where(t + t0 < n_ctx_tiles, ctx_row, b), 0, 0)),
            tok(POOL_W), hp, hn,
            pl.BlockSpec((1, t_, MLA_HEADS * MLA_V), lambda b, t: (b, t, 0)),
            tok(GLA_VW), tok(GLA_VW), tok(GLA_VW),
            _const_spec(gn.shape), _const_spec(wpool.shape), _const_spec(ps.shape), _const_spec(wo.shape),
            _const_spec(ones.shape),
        ],
        out_specs=pl.BlockSpec((1, t_, d), lambda b, t: (b, t, 0)),
        scratch_shapes=[pltpu.VMEM((t_ + 2 * HALO, POOL_W), F32)],
        compiler_params=_params(),
        name="post",
    )(xa, mod, pu, pu, pu, att, of, ob, og, gn, wpool, ps, wo, ones)


def _ffn_kernel(x_ref, xp_ref, xn_ref, mod_ref, n2_ref, wup_ref, cw_ref, cb_ref, wdn_ref, nf_ref, o_ref, ge_ref,
                *, n_ctx_tiles, n_tiles, final):
    t_ = TILE
    fc = D_FF // FF_CHUNKS
    ts, nseg = _segment(pl.program_id(1), n_ctx_tiles, n_tiles)
    mod = mod_ref[0]
    n2 = n2_ref[...]

    x = x_ref[0]
    xe = jnp.concatenate([xp_ref[0], x, xn_ref[0]], axis=0)
    hf = _rms(xe, n2) * (1.0 + mod[4:5]) + mod[3:4]
    he = hf.astype(BF16)
    h = hf[HALO:HALO + t_].astype(BF16)
    row = lax.broadcasted_iota(jnp.int32, (t_ + 2 * HALO, 1), 0)
    row_lo = jnp.where(ts > 0, 0, HALO)
    row_hi = jnp.where(ts < nseg - 1, t_ + 2 * HALO, t_ + HALO)
    y = jnp.zeros_like(x)
    for c in range(FF_CHUNKS):
        cu = slice(c * fc, (c + 1) * fc)
        cg = slice(D_FF + c * fc, D_FF + (c + 1) * fc)
        u = _dot(h, wup_ref[:, cu])
        g_ext = _dot(he, wup_ref[:, cg])
        ge_ref[...] = jnp.where(row >= row_lo, jnp.where(row < row_hi, g_ext, 0.0), 0.0)
        g = g_ext[HALO:HALO + t_]
        cw = cw_ref[:, cu]
        gc = (ge_ref[pl.ds(HALO - 1, t_), :] * cw[0:1] + g * cw[1:2]
              + ge_ref[pl.ds(HALO + 1, t_), :] * cw[2:3] + cb_ref[:, cu])
        y = y + _dot((_silu(gc) * u).astype(BF16), wdn_ref[cu, :])
    x2 = x + mod[5:6] * y
    o_ref[0] = _rms(x2, nf_ref[...]) if final else x2


def _ffn(x1, mod, n2, wup, cw, cb, wdn, nf, *, n_ctx_tiles, ctx_row, final):
    b_, s_, d = x1.shape
    t_ = TILE
    nt = s_ // t_
    hp, hn = _halo_specs(d, n_ctx_tiles, nt, 0)
    return pl.pallas_call(
        functools.partial(_ffn_kernel, n_ctx_tiles=n_ctx_tiles, n_tiles=nt, final=final),
        out_shape=jax.ShapeDtypeStruct((b_, s_, d), F32),
        grid=(b_, nt),
        in_specs=[
            pl.BlockSpec((1, t_, d), lambda b, t: (b, t, 0)), hp, hn,
            pl.BlockSpec((1, 6, d), lambda b, t: (jnp.where(t < n_ctx_tiles, ctx_row, b), 0, 0)),
            _const_spec(n2.shape), _const_spec(wup.shape), _const_spec(cw.shape), _const_spec(cb.shape),
            _const_spec(wdn.shape), _const_spec(nf.shape),
        ],
        out_specs=pl.BlockSpec((1, t_, d), lambda b, t: (b, t, 0)),
        scratch_shapes=[pltpu.VMEM((t_ + 2 * HALO, D_FF // FF_CHUNKS), F32)],
        compiler_params=_params(),
        name="ffn",
    )(x1, x1, x1, mod, n2, wup, cw, cb, wdn, nf)


def _rot_cols(w):
    w4 = w.reshape(w.shape[:-1] + (2, 2, MLA_ROPE // 4))
    return jnp.stack([-w4[..., 1, :], w4[..., 0, :]], axis=-2).reshape(w.shape)


def _prep_weights(w_in, w_uq, w_ukv, w_gk_f, b_gk_f, w_gk_b, b_gk_b, w_pool, gla_norm):
    depth, d, _ = w_in.shape
    z = lambda *s: jnp.zeros((depth,) + s, F32)
    o = 0
    parts = {}
    for name, size in (("ckv", KV_LORA), ("kr", MLA_ROPE), ("gk", GLA_KW), ("gv", GLA_VW),
                       ("lrf", GLA_GATE_RANK), ("lrb", GLA_GATE_RANK), ("pool", POOL_W), ("cq", Q_LORA),
                       ("gq", GLA_KW), ("og", GLA_VW)):
        parts[name] = w_in[..., o:o + size]
        o += size
    win = jnp.concatenate([
        parts["ckv"], parts["cq"], parts["pool"], parts["gq"], parts["gk"], parts["gv"], parts["og"],
        parts["lrf"], parts["lrb"], z(d, 32), parts["kr"], z(d, 32),
        z(d, 64), _rot_cols(parts["kr"]), z(d, 32)], axis=-1).astype(BF16)

    uq = w_uq.reshape(depth, Q_LORA, MLA_HEADS, MLA_QK)
    nope, rope = uq[..., :MLA_NOPE], uq[..., MLA_NOPE:]
    zq = lambda w: jnp.zeros((depth, Q_LORA, MLA_HEADS, w), F32)
    qa = jnp.concatenate([nope, rope, zq(32)], axis=-1).reshape(depth, Q_LORA, -1)
    qb = jnp.concatenate([zq(64), _rot_cols(rope), zq(32)], axis=-1).reshape(depth, Q_LORA, -1)
    wuqt = jnp.swapaxes(jnp.concatenate([qa, qb], axis=-1), 1, 2).astype(BF16)

    ukv = w_ukv.reshape(depth, KV_LORA, MLA_HEADS, MLA_NOPE + MLA_V)
    kn = jnp.concatenate([ukv[..., :MLA_NOPE], jnp.zeros((depth, KV_LORA, MLA_HEADS, 64), F32)], axis=-1)
    wuk = kn.reshape(depth, KV_LORA, -1).astype(BF16)
    wvt = jnp.swapaxes(ukv[..., MLA_NOPE:].reshape(depth, KV_LORA, -1), 1, 2).astype(BF16)

    r = GLA_GATE_RANK
    wg = jnp.zeros((depth, LANES, 2 * GLA_KW), F32)
    wg = wg.at[:, 0:r, 0:GLA_KW].set(w_gk_f).at[:, r:2 * r, GLA_KW:].set(w_gk_b).astype(BF16)
    bg = jnp.concatenate([b_gk_f, b_gk_b], axis=-1)[:, None, :]

    wpool = jnp.zeros((depth, POOL_W, POOL_W), F32)
    for g in range(POOL_GROUPS):
        sl = slice(g * POOL_GC, (g + 1) * POOL_GC)
        wpool = wpool.at[:, sl, sl].set(w_pool[:, g])
    gn = jnp.tile(gla_norm, (1, GLA_HEADS))[:, None, :]
    return win, wuqt, wuk, wvt, wg, bg, wpool.astype(BF16), gn


def _rope_tables(seq, ctx_len):
    rows = seq // GRID_W
    row = jnp.repeat(jnp.arange(rows), GRID_W).astype(F32)
    col = jnp.tile(jnp.arange(GRID_W), rows).astype(F32)
    half = MLA_ROPE // 2
    inv = ROPE_THETA ** (-jnp.arange(0, half, 2, dtype=F32) / half)
    ar = row[:, None] * inv
    ac = col[:, None] * inv
    ang = jnp.concatenate([ar, ar, ac, ac], axis=-1)
    cos = jnp.concatenate([jnp.ones((ctx_len, MLA_ROPE), F32), jnp.cos(ang)], axis=0)
    sin = jnp.concatenate([jnp.zeros((ctx_len, MLA_ROPE), F32), jnp.sin(ang)], axis=0)
    s_ = ctx_len + seq
    scale = MLA_QK ** -0.5 * LOG2E
    z = lambda w: jnp.zeros((s_, w), F32)
    tabk = jnp.concatenate([z(MLA_NOPE), cos, z(32), z(MLA_NOPE), sin, z(32)], axis=-1)
    tabq = jnp.concatenate([jnp.ones((s_, MLA_NOPE), F32) * scale, cos * scale, z(32),
                            z(MLA_NOPE), sin * scale, z(32)], axis=-1)
    return tabk, tabq.T


def kernel(x, c, ctx, c_ctx, w_ada, b_ada, norm1, norm2, w_in, q_norm, w_uq, kv_norm, w_ukv, w_gk_f, b_gk_f,
           w_gk_b, b_gk_b, gla_norm, w_pool, pool_scale, w_o, w_up, conv_w, conv_b, w_down, norm_f):
    b_, seq, d = x.shape
    ctx_len = ctx.shape[1]
    depth = w_ada.shape[0]
    assert seq % TILE == 0 and ctx_len % TILE == 0 and seq % GRID_W == 0
    nc = ctx_len // TILE

    bp = -(-(b_ + 1) // 8) * 8
    cvec = jnp.concatenate([c, c_ctx[None, :], jnp.zeros((bp - b_ - 1, d), F32)], axis=0)
    mod_all = _ada(cvec, w_ada, b_ada).reshape(depth, bp, 6, d)

    win, wuqt, wuk, wvt, wg, bg, wpool, gn = _prep_weights(w_in, w_uq, w_ukv, w_gk_f, b_gk_f, w_gk_b, b_gk_b,
                                                      w_pool, gla_norm)
    wo = w_o.astype(BF16)
    wup = w_up.astype(BF16)
    wdn = w_down.astype(BF16)
    tabk, tabq = _rope_tables(seq, ctx_len)
    gi = np.arange(GLA_VW) // GLA_DV
    ones = jnp.asarray((gi[:, None] == gi[None, :]).astype(np.float32), dtype=BF16)

    xa = jnp.concatenate([ctx, x], axis=1)
    for i in range(depth):
        last = i == depth - 1
        t0 = nc if last else 0
        mod = mod_all[i]
        q, k, v, pu, gqk, gg, gv, og = _pre(xa, mod, norm1[i][None], win[i], q_norm[i][None], wuqt[i],
                                            kv_norm[i][None], wuk[i], wvt[i], wg[i], bg[i], tabk, tabq,
                                            n_ctx_tiles=nc, ctx_row=b_)
        att = _attn(q, k, v, n_ctx_tiles=nc, t0=t0)
        of, ob = _gla(gqk, gg, gv, n_ctx_tiles=nc)
        x1 = _post(xa, mod, pu, att, of, ob, og, gn[i], wpool[i], pool_scale[i][None], wo[i], ones,
                   n_ctx_tiles=nc, ctx_row=b_, t0=t0)
        xa = _ffn(x1, mod, norm2[i][None], wup[i], conv_w[i], conv_b[i][None], wdn[i], norm_f[None],
                  n_ctx_tiles=0 if last else nc, ctx_row=b_, final=last)
    return xa
```

```python
import functools

import numpy as np
import jax
import jax.numpy as jnp
from jax import lax
from jax.experimental import pallas as pl
from jax.experimental.pallas import tpu as pltpu

F32 = jnp.float32
BF16 = jnp.bfloat16

EPS = 1e-6
LOG2E = 1.4426950408889634
GRID_W = 64
POOL_W = 256
POOL_GROUPS = 4
POOL_GC = POOL_W // POOL_GROUPS
POOL_WINDOWS = (2, 4, 8, 16)
MLA_HEADS = 8
MLA_NOPE = 64
MLA_ROPE = 32
MLA_V = 64
MLA_QK = MLA_NOPE + MLA_ROPE
Q_LORA = 256
KV_LORA = 128
ROPE_THETA = 10000.0
GLA_HEADS = 4
GLA_DK = 32
GLA_DV = 64
GLA_GATE_RANK = 16
GLA_GATE_NORM = 16.0
GLA_CHUNK = 64
GLA_KW = GLA_HEADS * GLA_DK
GLA_VW = GLA_HEADS * GLA_DV
D_FF = 2816

TILE = 256
HALO = 8
LANES = 128
HEAD_PAD = LANES
V_ROWS = 80
MIX_W = POOL_W + MLA_HEADS * MLA_V + GLA_VW
FF_CHUNKS = 2
SM_CHUNKS = 4
ATTN_UNROLL = 4
ATTN_TQ = 512
VMEM_LIMIT = 56 * 1024 * 1024

C_CKV = 0
C_CQ = C_CKV + KV_LORA
C_POOL = C_CQ + Q_LORA
C_GQK = C_POOL + POOL_W
C_GV = C_GQK + 2 * GLA_KW
C_OG = C_GV + GLA_VW
C_MISC_A = C_OG + GLA_VW
C_MISC_B = C_MISC_A + LANES
IN_COLS_P = C_MISC_B + LANES


def _dot(a, b):
    return jnp.dot(a, b, preferred_element_type=F32)


def _dot_nt(a, b):
    return lax.dot_general(a, b, (((1,), (1,)), ((), ())), preferred_element_type=F32)


def _dot_tn(a, b):
    return lax.dot_general(a, b, (((0,), (0,)), ((), ())), preferred_element_type=F32)


def _dot_exact_rhs(a, m):
    a1 = a.astype(BF16)
    r = a - a1.astype(F32)
    a2 = r.astype(BF16)
    a3 = (r - a2.astype(F32)).astype(BF16)
    return _dot(a1, m) + _dot(a2, m) + _dot(a3, m)


def _dot_exact_lhs(m, a):
    a1 = a.astype(BF16)
    r = a - a1.astype(F32)
    a2 = r.astype(BF16)
    a3 = (r - a2.astype(F32)).astype(BF16)
    return _dot(m, a1) + _dot(m, a2) + _dot(m, a3)


def _rms(x, g):
    return x * lax.rsqrt(jnp.mean(x * x, axis=-1, keepdims=True) + EPS) * g


def _silu(x):
    return x * jax.nn.sigmoid(x)


def _params():
    return pltpu.CompilerParams(vmem_limit_bytes=VMEM_LIMIT)


def _const_spec(shape):
    nd = len(shape)
    return pl.BlockSpec(shape, lambda *_: (0,) * nd, pipeline_mode=pl.Buffered(1))


def _ada_kernel(c_ref, w_ref, b_ref, o_ref):
    s = _silu(c_ref[...]).astype(BF16)
    o_ref[0] = _dot(s, w_ref[0].astype(BF16)) + b_ref[0]


def _ada(cvec, w_ada, b_ada):
    depth, d, n = w_ada.shape
    bp = cvec.shape[0]
    tn = n // 4
    return pl.pallas_call(
        _ada_kernel,
        out_shape=jax.ShapeDtypeStruct((depth, bp, n), F32),
        grid=(depth, n // tn),
        in_specs=[
            pl.BlockSpec((bp, d), lambda i, j: (0, 0)),
            pl.BlockSpec((1, d, tn), lambda i, j: (i, 0, j)),
            pl.BlockSpec((1, 1, tn), lambda i, j: (i, 0, j)),
        ],
        out_specs=pl.BlockSpec((1, bp, tn), lambda i, j: (i, 0, j)),
        compiler_params=_params(),
        name="ada",
    )(cvec, w_ada, b_ada.reshape(depth, 1, n))


def _pre_kernel(x_ref, mod_ref, n1_ref, win_ref, qn_ref, wuqt_ref, kvn_ref, wuk_ref, wvt_ref, wg_ref, bg_ref, tabk_ref,
                tabq_ref, qt_ref, k_ref, vt_ref, pu_ref, gqk_ref, gg_ref, gv_ref, og_ref):
    x = x_ref[0]
    mod = mod_ref[0]
    h = _rms(x, n1_ref[...]) * (1.0 + mod[1:2]) + mod[0:1]
    z = _dot(h.astype(BF16), win_ref[...])
    pu_ref[0] = z[:, C_POOL:C_POOL + POOL_W]
    gqk_ref[0] = z[:, C_GQK:C_GQK + 2 * GLA_KW]
    gv_ref[0] = z[:, C_GV:C_GV + GLA_VW]
    og_ref[0] = z[:, C_OG:C_OG + GLA_VW]
    za = z[:, C_MISC_A:C_MISC_A + LANES]
    zb = z[:, C_MISC_B:C_MISC_B + LANES]
    cosk, sink = tabk_ref[:, 0:LANES], tabk_ref[:, LANES:2 * LANES]

    a = _dot(za.astype(BF16), wg_ref[...]) + bg_ref[...]
    gg_ref[0] = (jnp.minimum(a, 0.0) - jnp.log1p(jnp.exp(-jnp.abs(a)))) * (1.0 / GLA_GATE_NORM)

    ckvn = _rms(z[:, C_CKV:C_CKV + KV_LORA], kvn_ref[...]).astype(BF16)
    uk = _dot(ckvn, wuk_ref[...])
    kr = za * cosk + zb * sink
    kw = MLA_HEADS * HEAD_PAD
    for hd in range(MLA_HEADS):
        sl = slice(hd * HEAD_PAD, (hd + 1) * HEAD_PAD)
        k_ref[0, :, sl] = (uk[:, sl] + kr).astype(BF16)
    vt = _dot_nt(wvt_ref[...], ckvn)
    pad_row = lax.broadcasted_iota(jnp.int32, (V_ROWS - MLA_V, x.shape[0]), 0)
    pad = jnp.where(pad_row == 0, 1.0, 0.0).astype(BF16)
    for hd in range(MLA_HEADS):
        vt_ref[0, hd, 0:MLA_V, :] = vt[hd * MLA_V:(hd + 1) * MLA_V].astype(BF16)
        vt_ref[0, hd, MLA_V:V_ROWS, :] = pad

    cqn = _rms(z[:, C_CQ:C_CQ + Q_LORA], qn_ref[...]).astype(BF16)
    uqt = _dot_nt(wuqt_ref[...], cqn)
    cosq, sinq = tabq_ref[0:LANES, :], tabq_ref[LANES:2 * LANES, :]
    for hd in range(MLA_HEADS):
        sl = slice(hd * HEAD_PAD, (hd + 1) * HEAD_PAD)
        sr = slice(kw + hd * HEAD_PAD, kw + (hd + 1) * HEAD_PAD)
        qt_ref[0, sl, :] = (uqt[sl] * cosq + uqt[sr] * sinq).astype(BF16)


def _pre(xa, mod, n1, win, qn, wuqt, kvn, wuk, wvt, wg, bg, tabk, tabq, *, n_ctx_tiles, ctx_row, q_pad):
    b_, s_, d = xa.shape
    t_ = TILE
    nt = s_ // t_
    tok = lambda w: pl.BlockSpec((1, t_, w), lambda b, t: (b, t, 0))
    outs = [(MLA_HEADS * HEAD_PAD, BF16), (POOL_W, F32), (2 * GLA_KW, F32), (2 * GLA_KW, F32), (GLA_VW, F32),
            (GLA_VW, F32)]
    assert q_pad % t_ == 0
    qt_shape = jax.ShapeDtypeStruct((b_, MLA_HEADS * HEAD_PAD, q_pad + s_), BF16)
    qt_spec = pl.BlockSpec((1, MLA_HEADS * HEAD_PAD, t_), lambda b, t: (b, 0, t + q_pad // t_))
    vt_shape = jax.ShapeDtypeStruct((b_, MLA_HEADS, V_ROWS, s_), BF16)
    vt_spec = pl.BlockSpec((1, MLA_HEADS, V_ROWS, t_), lambda b, t: (b, 0, 0, t))
    tok_shapes = [jax.ShapeDtypeStruct((b_, s_, w), dt) for w, dt in outs]
    tok_specs = [tok(w) for w, _ in outs]
    return pl.pallas_call(
        _pre_kernel,
        out_shape=[qt_shape, tok_shapes[0], vt_shape] + tok_shapes[1:],
        grid=(b_, nt),
        in_specs=[
            tok(d),
            pl.BlockSpec((1, 6, d), lambda b, t: (jnp.where(t < n_ctx_tiles, ctx_row, b), 0, 0)),
            _const_spec(n1.shape), _const_spec(win.shape), _const_spec(qn.shape), _const_spec(wuqt.shape),
            _const_spec(kvn.shape), _const_spec(wuk.shape), _const_spec(wvt.shape), _const_spec(wg.shape),
            _const_spec(bg.shape),
            pl.BlockSpec((t_, 2 * LANES), lambda b, t: (t, 0)),
            pl.BlockSpec((2 * LANES, t_), lambda b, t: (0, t)),
        ],
        out_specs=[qt_spec, tok_specs[0], vt_spec] + tok_specs[1:],
        compiler_params=_params(),
        name="pre",
    )(xa, mod, n1, win, qn, wuqt, kvn, wuk, wvt, wg, bg, tabk, tabq)


def _attn_kernel(qt_ref, k_ref, vt_ref, o_ref, m_ref, al_ref, acc_ref, s_ref, p_ref, *, n_kv):
    t_ = TILE
    nh = MLA_HEADS
    rc = t_ // SM_CHUNKS
    m_ref[...] = jnp.full(m_ref.shape, -jnp.inf, F32)
    al_ref[...] = jnp.ones(al_ref.shape, F32)
    acc_ref[...] = jnp.zeros(acc_ref.shape, F32)
    p_ref[...] = jnp.zeros(p_ref.shape, BF16)

    def scores(off, hd):
        sl = slice(hd * HEAD_PAD, (hd + 1) * HEAD_PAD)
        s_ref[hd] = _dot(k_ref[0, pl.ds(off, t_), sl], qt_ref[0, sl, :])

    def softmax(hd):
        mx = s_ref[hd, 0:rc, :]
        for c in range(1, SM_CHUNKS):
            mx = jnp.maximum(mx, s_ref[hd, c * rc:(c + 1) * rc, :])
        m_old = m_ref[hd:hd + 1, :]
        m_new = jnp.maximum(m_old, jnp.max(mx, axis=0, keepdims=True))
        for c in range(SM_CHUNKS):
            rows = slice(c * rc, (c + 1) * rc)
            p_ref[hd, rows, :] = jnp.exp2(s_ref[hd, rows, :] - m_new).astype(BF16)
        al_ref[hd:hd + 1, :] = jnp.exp2(m_old - m_new)
        m_ref[hd:hd + 1, :] = m_new

    def values(off, hd):
        pv = _dot(vt_ref[0, hd, :, pl.ds(off, t_)], p_ref[hd])
        acc_ref[hd] = acc_ref[hd] * al_ref[hd:hd + 1, :] + pv

    def tile_off(j):
        return pl.multiple_of(jnp.clip(j, 0, n_kv - 1) * t_, t_)

    def run_tiles(j0, count):
        items = [(tile_off(j0 + u), hd) for u in range(count) for hd in range(nh)]
        before = (tile_off(j0 - 1), nh - 1)
        after = (tile_off(j0 + count), 0)
        for i, (_, hd) in enumerate(items):
            scores(*(items[i + 1] if i + 1 < len(items) else after))
            values(*(items[i - 1] if i > 0 else before))
            softmax(hd)

    def group_body(g, carry):
        run_tiles(g * ATTN_UNROLL, ATTN_UNROLL)
        return carry

    def single_body(j, carry):
        run_tiles(j, 1)
        return carry

    scores(0, 0)
    n_grp = n_kv // ATTN_UNROLL
    lax.fori_loop(0, n_grp, group_body, 0)
    lax.fori_loop(n_grp * ATTN_UNROLL, n_kv, single_body, 0)
    values(tile_off(n_kv - 1), nh - 1)
    for pr in range(MLA_HEADS // 2):
        ot = jnp.concatenate(
            [acc_ref[hd, 0:MLA_V, :] / acc_ref[hd, MLA_V:MLA_V + 1, :] for hd in (2 * pr, 2 * pr + 1)], axis=0)
        o_ref[0, :, pr * 2 * MLA_V:(pr + 1) * 2 * MLA_V] = ot.T.astype(BF16)


def _attn(qt, k, vt, *, tq, q_start, n_q, n_keys):
    b_ = k.shape[0]
    t_ = TILE
    assert q_start % tq == 0 and n_keys % t_ == 0
    q0 = q_start // tq
    return pl.pallas_call(
        functools.partial(_attn_kernel, n_kv=n_keys // t_),
        out_shape=jax.ShapeDtypeStruct((b_, n_q * tq, MLA_HEADS * MLA_V), BF16),
        grid=(b_, n_q),
        in_specs=[
            pl.BlockSpec((1, MLA_HEADS * HEAD_PAD, tq), lambda b, t: (b, 0, t + q0)),
            pl.BlockSpec((1, n_keys, MLA_HEADS * HEAD_PAD), lambda b, t: (b, 0, 0)),
            pl.BlockSpec((1, MLA_HEADS, V_ROWS, n_keys), lambda b, t: (b, 0, 0, 0)),
        ],
        out_specs=pl.BlockSpec((1, tq, MLA_HEADS * MLA_V), lambda b, t: (b, t, 0)),
        scratch_shapes=[pltpu.VMEM((MLA_HEADS, tq), F32), pltpu.VMEM((MLA_HEADS, tq), F32),
                        pltpu.VMEM((MLA_HEADS, V_ROWS, tq), F32),
                        pltpu.VMEM((MLA_HEADS, t_, tq), F32), pltpu.VMEM((MLA_HEADS, t_, tq), BF16)],
        compiler_params=_params(),
        name="attn",
    )(qt, k, vt)


def _gla_kernel(qkf_ref, gf_ref, vf_ref, qkb_ref, gb_ref, vb_ref, mstf_ref, mstb_ref, of_ref, ob_ref, stf_ref,
                stb_ref):
    @pl.when(pl.program_id(1) == 0)
    def _():
        stf_ref[...] = jnp.zeros_like(stf_ref)
        stb_ref[...] = jnp.zeros_like(stb_ref)

    _gla_direction(qkf_ref, gf_ref, vf_ref, mstf_ref, of_ref, stf_ref, rev=False)
    _gla_direction(qkb_ref, gb_ref, vb_ref, mstb_ref, ob_ref, stb_ref, rev=True)


def _gla_direction(qk_ref, g_ref, v_ref, mst_ref, o_ref, st_ref, *, rev):
    t_ = TILE
    ch = GLA_CHUNK

    q = qk_ref[0, :, 0:GLA_KW] * (GLA_DK ** -0.5)
    k = qk_ref[0, :, GLA_KW:2 * GLA_KW]
    v = v_ref[0].astype(BF16)
    b = _dot_exact_lhs(mst_ref[...], g_ref[0])
    end_row = 0 if rev else ch - 1
    mid_row = ch - 1 - ch // 2 if rev else ch // 2

    def chunk_rows(r):
        return jnp.concatenate(
            [jnp.broadcast_to(b[c * ch + r:c * ch + r + 1], (ch, GLA_KW)) for c in range(t_ // ch)], axis=0)

    bl, mid = chunk_rows(end_row), chunk_rows(mid_row)
    kw = (k * jnp.exp(bl - b)).astype(BF16)
    qe = (q * jnp.exp(b)).astype(BF16)
    q2 = q * jnp.exp(b - mid)
    k2 = (k * jnp.exp(mid - b)).astype(BF16)

    ri = lax.broadcasted_iota(jnp.int32, (t_, t_), 0)
    ci = lax.broadcasted_iota(jnp.int32, (t_, t_), 1)
    same = (ri >> 6) == (ci >> 6)
    tri = (ri <= ci) if rev else (ri >= ci)
    lane_k = lax.broadcasted_iota(jnp.int32, (1, GLA_KW), 1)
    lane_v = lax.broadcasted_iota(jnp.int32, (1, GLA_VW), 1)
    o_intra = jnp.zeros((t_, GLA_VW), F32)
    for hd in range(GLA_HEADS):
        qh = jnp.where((lane_k >> 5) == hd, q2, 0.0).astype(BF16)
        att = _dot_nt(qh, k2)
        att = jnp.where(same, jnp.where(tri, att, 0.0), 0.0).astype(BF16)
        o_intra = o_intra + jnp.where((lane_v >> 6) == hd, _dot(att, v), 0.0)

    rs = lax.broadcasted_iota(jnp.int32, (GLA_VW, GLA_KW), 0)
    cs_ = lax.broadcasted_iota(jnp.int32, (GLA_VW, GLA_KW), 1)
    bd = (rs >> 6) == (cs_ >> 5)
    st = st_ref[...]
    order = range(t_ // ch - 1, -1, -1) if rev else range(t_ // ch)
    for c in order:
        rows = slice(c * ch, (c + 1) * ch)
        o_ref[0, rows, :] = o_intra[rows] + _dot_nt(qe[rows], st.astype(BF16))
        dec = jnp.exp(bl[c * ch:c * ch + 1])
        st = dec * st + jnp.where(bd, _dot_tn(v[rows], kw[rows]), 0.0)
    st_ref[...] = st


def _gla_consts(rev):
    t_, ch = TILE, GLA_CHUNK
    i = np.arange(t_)[:, None]
    j = np.arange(t_)[None, :]
    same = (i // ch) == (j // ch)
    cum = same & ((j >= i) if rev else (j <= i))
    return jnp.asarray(cum.astype(np.float32), dtype=BF16)


def _gla(gqk, gg, gv, *, n_ctx_tiles):
    b_, s_, _ = gqk.shape
    t_ = TILE
    nt = s_ // t_
    fwd = lambda t: t
    bwd = lambda t: jnp.where(t < n_ctx_tiles, n_ctx_tiles - 1 - t, nt - 1 - (t - n_ctx_tiles))
    mstf, mstb = _gla_consts(False), _gla_consts(True)

    def specs(tile, d):
        return [pl.BlockSpec((1, t_, 2 * GLA_KW), lambda b, t: (b, tile(t), 0)),
                pl.BlockSpec((1, t_, GLA_KW), lambda b, t: (b, tile(t), d)),
                pl.BlockSpec((1, t_, GLA_VW), lambda b, t: (b, tile(t), 0))]

    out = jax.ShapeDtypeStruct((b_, s_, GLA_VW), F32)
    return pl.pallas_call(
        _gla_kernel,
        out_shape=[out, out],
        grid=(b_, nt),
        in_specs=specs(fwd, 0) + specs(bwd, 1) + [_const_spec(mstf.shape), _const_spec(mstb.shape)],
        out_specs=[pl.BlockSpec((1, t_, GLA_VW), lambda b, t: (b, fwd(t), 0)),
                   pl.BlockSpec((1, t_, GLA_VW), lambda b, t: (b, bwd(t), 0))],
        scratch_shapes=[pltpu.VMEM((GLA_VW, GLA_KW), F32), pltpu.VMEM((GLA_VW, GLA_KW), F32)],
        compiler_params=_params(),
        name="gla",
    )(gqk, gg, gv, gqk, gg, gv, mstf, mstb)


def _segment(t, n_ctx_tiles, n_tiles):
    in_ctx = t < n_ctx_tiles
    return jnp.where(in_ctx, t, t - n_ctx_tiles), jnp.where(in_ctx, n_ctx_tiles, n_tiles - n_ctx_tiles)


def _halo_specs(width, n_ctx_tiles, n_tiles, t0):
    per = TILE // HALO

    def prev(b, t):
        tt = t + t0
        ts, _ = _segment(tt, n_ctx_tiles, n_tiles)
        return (b, jnp.where(ts == 0, tt * per, tt * per - 1), 0)

    def nxt(b, t):
        tt = t + t0
        ts, n = _segment(tt, n_ctx_tiles, n_tiles)
        return (b, jnp.where(ts == n - 1, (tt + 1) * per - 1, (tt + 1) * per), 0)

    return pl.BlockSpec((1, HALO, width), prev), pl.BlockSpec((1, HALO, width), nxt)


def _post_kernel(x_ref, mod_ref, pu_ref, pp_ref, pn_ref, attc_ref, attl_ref, of_ref, ob_ref, og_ref, gn_ref, wpool_ref,
                 ps_ref, wo_ref, ones_ref, o_ref, ue_ref, *, n_ctx_tiles, n_tiles, t0):
    t_ = TILE
    tt = pl.program_id(1) + t0
    ts, nseg = _segment(tt, n_ctx_tiles, n_tiles)
    seg_len = nseg * t_

    ue_ref[0:HALO] = jnp.where(ts > 0, pp_ref[0], 0.0)
    ue_ref[HALO:HALO + t_] = pu_ref[0]
    ue_ref[HALO + t_:2 * HALO + t_] = jnp.where(ts < nseg - 1, pn_ref[0], 0.0)
    pos = ts * t_ + lax.broadcasted_iota(jnp.int32, (t_, 1), 0)
    lane = lax.broadcasted_iota(jnp.int32, (1, LANES), 1)

    def shifted(offsets, c0):
        acc = None
        for o in offsets:
            r = ue_ref[pl.ds(HALO + o, t_), c0:c0 + LANES]
            acc = r if acc is None else acc + r
        return acc

    def count(w):
        lo = pos - w // 2
        return (jnp.minimum(lo + w, seg_len) - jnp.maximum(lo, 0)).astype(F32)

    w2 = shifted((-1, 0), 0)
    w4 = w2 + shifted((-2, 1), 0)
    w8 = shifted(range(-4, 4), LANES)
    w16 = w8 + shifted(tuple(range(-8, -4)) + tuple(range(4, 8)), LANES)
    u = pu_ref[0]
    d01 = jnp.where(lane < POOL_GC, w2 / count(2), w4 / count(4)) - u[:, 0:LANES]
    d23 = jnp.where(lane < POOL_GC, w8 / count(8), w16 / count(16)) - u[:, LANES:2 * LANES]
    diff = jnp.concatenate([d01, d23], axis=1).astype(BF16)
    ypool = _dot(diff, wpool_ref[...]) * ps_ref[...]

    o = of_ref[0] + ob_ref[0]
    ms = _dot_exact_rhs(o * o, ones_ref[...]) * (1.0 / GLA_DV)
    yg = o * lax.rsqrt(ms + EPS) * gn_ref[...] * _silu(og_ref[0])

    att = jnp.where(tt < n_ctx_tiles, attc_ref[0], attl_ref[0])
    ycat = jnp.concatenate([ypool.astype(BF16), att, yg.astype(BF16)], axis=1)
    o_ref[0] = x_ref[0] + mod_ref[0][2:3] * _dot(ycat, wo_ref[...])


def _post(xa, mod, pu, att_c, att_l, of, ob, og, gn, wpool, ps, wo, ones, *, n_ctx_tiles, ctx_row, t0):
    b_, s_, d = xa.shape
    t_ = TILE
    nt = s_ // t_
    tok = lambda w: pl.BlockSpec((1, t_, w), lambda b, t: (b, t + t0, 0))
    hp, hn = _halo_specs(POOL_W, n_ctx_tiles, nt, t0)
    return pl.pallas_call(
        functools.partial(_post_kernel, n_ctx_tiles=n_ctx_tiles, n_tiles=nt, t0=t0),
        out_shape=jax.ShapeDtypeStruct((b_, (nt - t0) * t_, d), F32),
        grid=(b_, nt - t0),
        in_specs=[
            tok(d),
            pl.BlockSpec((1, 6, d), lambda b, t: (jnp.where(t + t0 < n_ctx_tiles, ctx_row, b), 0, 0)),
            tok(POOL_W), hp, hn,
            pl.BlockSpec((1, t_, MLA_HEADS * MLA_V), lambda b, t: (b, jnp.minimum(t + t0, max(n_ctx_tiles - 1, 0)), 0)),
            pl.BlockSpec((1, t_, MLA_HEADS * MLA_V), lambda b, t: (b, jnp.maximum(t + t0 - n_ctx_tiles, 0), 0)),
            tok(GLA_VW), tok(GLA_VW), tok(GLA_VW),
            _const_spec(gn.shape), _const_spec(wpool.shape), _const_spec(ps.shape), _const_spec(wo.shape),
            _const_spec(ones.shape),
        ],
        out_specs=pl.BlockSpec((1, t_, d), lambda b, t: (b, t, 0)),
        scratch_shapes=[pltpu.VMEM((t_ + 2 * HALO, POOL_W), F32)],
        compiler_params=_params(),
        name="post",
    )(xa, mod, pu, pu, pu, att_c, att_l, of, ob, og, gn, wpool, ps, wo, ones)


def _ffn_kernel(x_ref, xp_ref, xn_ref, mod_ref, n2_ref, wup_ref, cw_ref, cb_ref, wdn_ref, nf_ref, o_ref, ge_ref,
                *, n_ctx_tiles, n_tiles, final):
    t_ = TILE
    fc = D_FF // FF_CHUNKS
    ts, nseg = _segment(pl.program_id(1), n_ctx_tiles, n_tiles)
    mod = mod_ref[0]
    n2 = n2_ref[...]

    x = x_ref[0]
    xe = jnp.concatenate([xp_ref[0], x, xn_ref[0]], axis=0)
    hf = _rms(xe, n2) * (1.0 + mod[4:5]) + mod[3:4]
    he = hf.astype(BF16)
    h = hf[HALO:HALO + t_].astype(BF16)
    row = lax.broadcasted_iota(jnp.int32, (t_ + 2 * HALO, 1), 0)
    row_lo = jnp.where(ts > 0, 0, HALO)
    row_hi = jnp.where(ts < nseg - 1, t_ + 2 * HALO, t_ + HALO)
    y = jnp.zeros_like(x)
    for c in range(FF_CHUNKS):
        cu = slice(c * fc, (c + 1) * fc)
        cg = slice(D_FF + c * fc, D_FF + (c + 1) * fc)
        u = _dot(h, wup_ref[:, cu])
        g_ext = _dot(he, wup_ref[:, cg])
        ge_ref[...] = jnp.where(row >= row_lo, jnp.where(row < row_hi, g_ext, 0.0), 0.0)
        g = g_ext[HALO:HALO + t_]
        cw = cw_ref[:, cu]
        gc = (ge_ref[pl.ds(HALO - 1, t_), :] * cw[0:1] + g * cw[1:2]
              + ge_ref[pl.ds(HALO + 1, t_), :] * cw[2:3] + cb_ref[:, cu])
        y = y + _dot((_silu(gc) * u).astype(BF16), wdn_ref[cu, :])
    x2 = x + mod[5:6] * y
    o_ref[0] = _rms(x2, nf_ref[...]) if final else x2


def _ffn(x1, mod, n2, wup, cw, cb, wdn, nf, *, n_ctx_tiles, ctx_row, final):
    b_, s_, d = x1.shape
    t_ = TILE
    nt = s_ // t_
    hp, hn = _halo_specs(d, n_ctx_tiles, nt, 0)
    return pl.pallas_call(
        functools.partial(_ffn_kernel, n_ctx_tiles=n_ctx_tiles, n_tiles=nt, final=final),
        out_shape=jax.ShapeDtypeStruct((b_, s_, d), F32),
        grid=(b_, nt),
        in_specs=[
            pl.BlockSpec((1, t_, d), lambda b, t: (b, t, 0)), hp, hn,
            pl.BlockSpec((1, 6, d), lambda b, t: (jnp.where(t < n_ctx_tiles, ctx_row, b), 0, 0)),
            _const_spec(n2.shape), _const_spec(wup.shape), _const_spec(cw.shape), _const_spec(cb.shape),
            _const_spec(wdn.shape), _const_spec(nf.shape),
        ],
        out_specs=pl.BlockSpec((1, t_, d), lambda b, t: (b, t, 0)),
        scratch_shapes=[pltpu.VMEM((t_ + 2 * HALO, D_FF // FF_CHUNKS), F32)],
        compiler_params=_params(),
        name="ffn",
    )(x1, x1, x1, mod, n2, wup, cw, cb, wdn, nf)


def _rot_cols(w):
    w4 = w.reshape(w.shape[:-1] + (2, 2, MLA_ROPE // 4))
    return jnp.stack([-w4[..., 1, :], w4[..., 0, :]], axis=-2).reshape(w.shape)


def _prep_weights(w_in, w_uq, w_ukv, w_gk_f, b_gk_f, w_gk_b, b_gk_b, w_pool, gla_norm):
    depth, d, _ = w_in.shape
    z = lambda *s: jnp.zeros((depth,) + s, F32)
    o = 0
    parts = {}
    for name, size in (("ckv", KV_LORA), ("kr", MLA_ROPE), ("gk", GLA_KW), ("gv", GLA_VW),
                       ("lrf", GLA_GATE_RANK), ("lrb", GLA_GATE_RANK), ("pool", POOL_W), ("cq", Q_LORA),
                       ("gq", GLA_KW), ("og", GLA_VW)):
        parts[name] = w_in[..., o:o + size]
        o += size
    win = jnp.concatenate([
        parts["ckv"], parts["cq"], parts["pool"], parts["gq"], parts["gk"], parts["gv"], parts["og"],
        parts["lrf"], parts["lrb"], z(d, 32), parts["kr"], z(d, 32),
        z(d, 64), _rot_cols(parts["kr"]), z(d, 32)], axis=-1).astype(BF16)

    uq = w_uq.reshape(depth, Q_LORA, MLA_HEADS, MLA_QK)
    nope, rope = uq[..., :MLA_NOPE], uq[..., MLA_NOPE:]
    zq = lambda w: jnp.zeros((depth, Q_LORA, MLA_HEADS, w), F32)
    qa = jnp.concatenate([nope, rope, zq(32)], axis=-1).reshape(depth, Q_LORA, -1)
    qb = jnp.concatenate([zq(64), _rot_cols(rope), zq(32)], axis=-1).reshape(depth, Q_LORA, -1)
    wuqt = jnp.swapaxes(jnp.concatenate([qa, qb], axis=-1), 1, 2).astype(BF16)

    ukv = w_ukv.reshape(depth, KV_LORA, MLA_HEADS, MLA_NOPE + MLA_V)
    kn = jnp.concatenate([ukv[..., :MLA_NOPE], jnp.zeros((depth, KV_LORA, MLA_HEADS, 64), F32)], axis=-1)
    wuk = kn.reshape(depth, KV_LORA, -1).astype(BF16)
    wvt = jnp.swapaxes(ukv[..., MLA_NOPE:].reshape(depth, KV_LORA, -1), 1, 2).astype(BF16)

    r = GLA_GATE_RANK
    wg = jnp.zeros((depth, LANES, 2 * GLA_KW), F32)
    wg = wg.at[:, 0:r, 0:GLA_KW].set(w_gk_f).at[:, r:2 * r, GLA_KW:].set(w_gk_b).astype(BF16)
    bg = jnp.concatenate([b_gk_f, b_gk_b], axis=-1)[:, None, :]

    wpool = jnp.zeros((depth, POOL_W, POOL_W), F32)
    for g in range(POOL_GROUPS):
        sl = slice(g * POOL_GC, (g + 1) * POOL_GC)
        wpool = wpool.at[:, sl, sl].set(w_pool[:, g])
    gn = jnp.tile(gla_norm, (1, GLA_HEADS))[:, None, :]
    return win, wuqt, wuk, wvt, wg, bg, wpool.astype(BF16), gn


def _rope_tables(seq, ctx_len):
    rows = seq // GRID_W
    row = jnp.repeat(jnp.arange(rows), GRID_W).astype(F32)
    col = jnp.tile(jnp.arange(GRID_W), rows).astype(F32)
    half = MLA_ROPE // 2
    inv = ROPE_THETA ** (-jnp.arange(0, half, 2, dtype=F32) / half)
    ar = row[:, None] * inv
    ac = col[:, None] * inv
    ang = jnp.concatenate([ar, ar, ac, ac], axis=-1)
    cos = jnp.concatenate([jnp.ones((ctx_len, MLA_ROPE), F32), jnp.cos(ang)], axis=0)
    sin = jnp.concatenate([jnp.zeros((ctx_len, MLA_ROPE), F32), jnp.sin(ang)], axis=0)
    s_ = ctx_len + seq
    scale = MLA_QK ** -0.5 * LOG2E
    z = lambda w: jnp.zeros((s_, w), F32)
    tabk = jnp.concatenate([z(MLA_NOPE), cos, z(32), z(MLA_NOPE), sin, z(32)], axis=-1)
    tabq = jnp.concatenate([jnp.ones((s_, MLA_NOPE), F32) * scale, cos * scale, z(32),
                            z(MLA_NOPE), sin * scale, z(32)], axis=-1)
    return tabk, tabq.T


def kernel(x, c, ctx, c_ctx, w_ada, b_ada, norm1, norm2, w_in, q_norm, w_uq, kv_norm, w_ukv, w_gk_f, b_gk_f,
           w_gk_b, b_gk_b, gla_norm, w_pool, pool_scale, w_o, w_up, conv_w, conv_b, w_down, norm_f):
    b_, seq, d = x.shape
    ctx_len = ctx.shape[1]
    depth = w_ada.shape[0]
    assert seq % TILE == 0 and ctx_len % TILE == 0 and seq % GRID_W == 0
    nc = ctx_len // TILE

    bp = -(-(b_ + 1) // 8) * 8
    cvec = jnp.concatenate([c, c_ctx[None, :], jnp.zeros((bp - b_ - 1, d), F32)], axis=0)
    mod_all = _ada(cvec, w_ada, b_ada).reshape(depth, bp, 6, d)

    win, wuqt, wuk, wvt, wg, bg, wpool, gn = _prep_weights(w_in, w_uq, w_ukv, w_gk_f, b_gk_f, w_gk_b, b_gk_b,
                                                      w_pool, gla_norm)
    wo = w_o.astype(BF16)
    wup = w_up.astype(BF16)
    wdn = w_down.astype(BF16)
    tabk, tabq = _rope_tables(seq, ctx_len)
    gi = np.arange(GLA_VW) // GLA_DV
    ones = jnp.asarray((gi[:, None] == gi[None, :]).astype(np.float32), dtype=BF16)

    xa = jnp.concatenate([ctx, x], axis=1)
    tq = ATTN_TQ if seq % ATTN_TQ == 0 else TILE
    q_pad = -ctx_len % tq
    for i in range(depth):
        last = i == depth - 1
        t0 = nc if last else 0
        mod = mod_all[i]
        q, k, v, pu, gqk, gg, gv, og = _pre(xa, mod, norm1[i][None], win[i], q_norm[i][None], wuqt[i],
                                            kv_norm[i][None], wuk[i], wvt[i], wg[i], bg[i], tabk, tabq,
                                            n_ctx_tiles=nc, ctx_row=b_, q_pad=q_pad)
        att_l = _attn(q, k, v, tq=tq, q_start=q_pad + ctx_len, n_q=seq // tq, n_keys=ctx_len + seq)
        att_c = att_l if last else _attn(q, k, v, tq=TILE, q_start=q_pad, n_q=nc, n_keys=ctx_len)
        of, ob = _gla(gqk, gg, gv, n_ctx_tiles=nc)
        x1 = _post(xa, mod, pu, att_c, att_l, of, ob, og, gn[i], wpool[i], pool_scale[i][None], wo[i], ones,
                   n_ctx_tiles=nc, ctx_row=b_, t0=t0)
        xa = _ffn(x1, mod, norm2[i][None], wup[i], conv_w[i], conv_b[i][None], wdn[i], norm_f[None],
                  n_ctx_tiles=0 if last else nc, ctx_row=b_, final=last)
    return xa
```

```python
import functools

import numpy as np
import jax
import jax.numpy as jnp
from jax import lax
from jax.experimental import pallas as pl
from jax.experimental.pallas import tpu as pltpu

F32 = jnp.float32
BF16 = jnp.bfloat16

EPS = 1e-6
LOG2E = 1.4426950408889634
GRID_W = 64
POOL_W = 256
POOL_GROUPS = 4
POOL_GC = POOL_W // POOL_GROUPS
POOL_WINDOWS = (2, 4, 8, 16)
MLA_HEADS = 8
MLA_NOPE = 64
MLA_ROPE = 32
MLA_V = 64
MLA_QK = MLA_NOPE + MLA_ROPE
Q_LORA = 256
KV_LORA = 128
ROPE_THETA = 10000.0
GLA_HEADS = 4
GLA_DK = 32
GLA_DV = 64
GLA_GATE_RANK = 16
GLA_GATE_NORM = 16.0
GLA_CHUNK = 64
GLA_KW = GLA_HEADS * GLA_DK
GLA_VW = GLA_HEADS * GLA_DV
D_FF = 2816

TILE = 256
HALO = 8
LANES = 128
HEAD_PAD = LANES
V_ROWS = 80
MIX_W = POOL_W + MLA_HEADS * MLA_V + GLA_VW
FF_CHUNKS = 1
FF_BATCH = 2
SM_CHUNKS = 4
ATTN_UNROLL = 4
ATTN_TQ = 512
VMEM_LIMIT = 56 * 1024 * 1024

C_CKV = 0
C_CQ = C_CKV + KV_LORA
C_POOL = C_CQ + Q_LORA
C_GQK = C_POOL + POOL_W
C_GV = C_GQK + 2 * GLA_KW
C_OG = C_GV + GLA_VW
C_MISC_A = C_OG + GLA_VW
C_MISC_B = C_MISC_A + LANES
IN_COLS_P = C_MISC_B + LANES


def _dot(a, b):
    return jnp.dot(a, b, preferred_element_type=F32)


def _dot_nt(a, b):
    return lax.dot_general(a, b, (((1,), (1,)), ((), ())), preferred_element_type=F32)


def _dot_tn(a, b):
    return lax.dot_general(a, b, (((0,), (0,)), ((), ())), preferred_element_type=F32)


def _dot_exact_rhs(a, m):
    a1 = a.astype(BF16)
    r = a - a1.astype(F32)
    a2 = r.astype(BF16)
    a3 = (r - a2.astype(F32)).astype(BF16)
    return _dot(a1, m) + _dot(a2, m) + _dot(a3, m)


def _dot_exact_lhs(m, a):
    a1 = a.astype(BF16)
    r = a - a1.astype(F32)
    a2 = r.astype(BF16)
    a3 = (r - a2.astype(F32)).astype(BF16)
    return _dot(m, a1) + _dot(m, a2) + _dot(m, a3)


def _rms(x, g):
    return x * lax.rsqrt(jnp.mean(x * x, axis=-1, keepdims=True) + EPS) * g


def _silu(x):
    return x * jax.nn.sigmoid(x)


def _params():
    return pltpu.CompilerParams(vmem_limit_bytes=VMEM_LIMIT)


def _const_spec(shape):
    nd = len(shape)
    return pl.BlockSpec(shape, lambda *_: (0,) * nd, pipeline_mode=pl.Buffered(1))


def _ada_kernel(c_ref, w_ref, b_ref, o_ref):
    s = _silu(c_ref[...]).astype(BF16)
    o_ref[0] = _dot(s, w_ref[0].astype(BF16)) + b_ref[0]


def _ada(cvec, w_ada, b_ada):
    depth, d, n = w_ada.shape
    bp = cvec.shape[0]
    tn = n // 4
    return pl.pallas_call(
        _ada_kernel,
        out_shape=jax.ShapeDtypeStruct((depth, bp, n), F32),
        grid=(depth, n // tn),
        in_specs=[
            pl.BlockSpec((bp, d), lambda i, j: (0, 0)),
            pl.BlockSpec((1, d, tn), lambda i, j: (i, 0, j)),
            pl.BlockSpec((1, 1, tn), lambda i, j: (i, 0, j)),
        ],
        out_specs=pl.BlockSpec((1, bp, tn), lambda i, j: (i, 0, j)),
        compiler_params=_params(),
        name="ada",
    )(cvec, w_ada, b_ada.reshape(depth, 1, n))


def _pre_kernel(x_ref, mod_ref, n1_ref, win_ref, qn_ref, wuqt_ref, kvn_ref, wuk_ref, wvt_ref, wg_ref, bg_ref, tabk_ref,
                tabq_ref, qt_ref, k_ref, vt_ref, pu_ref, gqk_ref, gg_ref, gv_ref, og_ref):
    x = x_ref[0]
    mod = mod_ref[0]
    h = _rms(x, n1_ref[...]) * (1.0 + mod[1:2]) + mod[0:1]
    z = _dot(h.astype(BF16), win_ref[...])
    pu_ref[0] = z[:, C_POOL:C_POOL + POOL_W]
    gqk_ref[0] = z[:, C_GQK:C_GQK + 2 * GLA_KW]
    gv_ref[0] = z[:, C_GV:C_GV + GLA_VW]
    og_ref[0] = z[:, C_OG:C_OG + GLA_VW]
    za = z[:, C_MISC_A:C_MISC_A + LANES]
    zb = z[:, C_MISC_B:C_MISC_B + LANES]
    cosk, sink = tabk_ref[:, 0:LANES], tabk_ref[:, LANES:2 * LANES]

    a = _dot(za.astype(BF16), wg_ref[...]) + bg_ref[...]
    gg_ref[0] = (jnp.minimum(a, 0.0) - jnp.log1p(jnp.exp(-jnp.abs(a)))) * (1.0 / GLA_GATE_NORM)

    ckvn = _rms(z[:, C_CKV:C_CKV + KV_LORA], kvn_ref[...]).astype(BF16)
    uk = _dot(ckvn, wuk_ref[...])
    kr = za * cosk + zb * sink
    kw = MLA_HEADS * HEAD_PAD
    for hd in range(MLA_HEADS):
        sl = slice(hd * HEAD_PAD, (hd + 1) * HEAD_PAD)
        k_ref[0, :, sl] = (uk[:, sl] + kr).astype(BF16)
    vt = _dot_nt(wvt_ref[...], ckvn)
    pad_row = lax.broadcasted_iota(jnp.int32, (V_ROWS - MLA_V, x.shape[0]), 0)
    pad = jnp.where(pad_row == 0, 1.0, 0.0).astype(BF16)
    for hd in range(MLA_HEADS):
        vt_ref[0, hd, 0:MLA_V, :] = vt[hd * MLA_V:(hd + 1) * MLA_V].astype(BF16)
        vt_ref[0, hd, MLA_V:V_ROWS, :] = pad

    cqn = _rms(z[:, C_CQ:C_CQ + Q_LORA], qn_ref[...]).astype(BF16)
    uqt = _dot_nt(wuqt_ref[...], cqn)
    cosq, sinq = tabq_ref[0:LANES, :], tabq_ref[LANES:2 * LANES, :]
    for hd in range(MLA_HEADS):
        sl = slice(hd * HEAD_PAD, (hd + 1) * HEAD_PAD)
        sr = slice(kw + hd * HEAD_PAD, kw + (hd + 1) * HEAD_PAD)
        qt_ref[0, sl, :] = (uqt[sl] * cosq + uqt[sr] * sinq).astype(BF16)


def _pre(xa, mod, n1, win, qn, wuqt, kvn, wuk, wvt, wg, bg, tabk, tabq, *, n_ctx_tiles, ctx_row, q_pad):
    b_, s_, d = xa.shape
    t_ = TILE
    nt = s_ // t_
    tok = lambda w: pl.BlockSpec((1, t_, w), lambda b, t: (b, t, 0))
    outs = [(MLA_HEADS * HEAD_PAD, BF16), (POOL_W, F32), (2 * GLA_KW, F32), (2 * GLA_KW, F32), (GLA_VW, F32),
            (GLA_VW, F32)]
    assert q_pad % t_ == 0
    qt_shape = jax.ShapeDtypeStruct((b_, MLA_HEADS * HEAD_PAD, q_pad + s_), BF16)
    qt_spec = pl.BlockSpec((1, MLA_HEADS * HEAD_PAD, t_), lambda b, t: (b, 0, t + q_pad // t_))
    vt_shape = jax.ShapeDtypeStruct((b_, MLA_HEADS, V_ROWS, s_), BF16)
    vt_spec = pl.BlockSpec((1, MLA_HEADS, V_ROWS, t_), lambda b, t: (b, 0, 0, t))
    tok_shapes = [jax.ShapeDtypeStruct((b_, s_, w), dt) for w, dt in outs]
    tok_specs = [tok(w) for w, _ in outs]
    return pl.pallas_call(
        _pre_kernel,
        out_shape=[qt_shape, tok_shapes[0], vt_shape] + tok_shapes[1:],
        grid=(b_, nt),
        in_specs=[
            tok(d),
            pl.BlockSpec((1, 6, d), lambda b, t: (jnp.where(t < n_ctx_tiles, ctx_row, b), 0, 0)),
            _const_spec(n1.shape), _const_spec(win.shape), _const_spec(qn.shape), _const_spec(wuqt.shape),
            _const_spec(kvn.shape), _const_spec(wuk.shape), _const_spec(wvt.shape), _const_spec(wg.shape),
            _const_spec(bg.shape),
            pl.BlockSpec((t_, 2 * LANES), lambda b, t: (t, 0)),
            pl.BlockSpec((2 * LANES, t_), lambda b, t: (0, t)),
        ],
        out_specs=[qt_spec, tok_specs[0], vt_spec] + tok_specs[1:],
        compiler_params=_params(),
        name="pre",
    )(xa, mod, n1, win, qn, wuqt, kvn, wuk, wvt, wg, bg, tabk, tabq)


def _attn_kernel(qt_ref, k_ref, vt_ref, o_ref, m_ref, al_ref, acc_ref, s_ref, p_ref, *, n_kv):
    t_ = TILE
    nh = MLA_HEADS
    rc = t_ // SM_CHUNKS
    m_ref[...] = jnp.full(m_ref.shape, -jnp.inf, F32)
    al_ref[...] = jnp.ones(al_ref.shape, F32)
    acc_ref[...] = jnp.zeros(acc_ref.shape, F32)
    p_ref[...] = jnp.zeros(p_ref.shape, BF16)

    def scores(off, hd):
        sl = slice(hd * HEAD_PAD, (hd + 1) * HEAD_PAD)
        s_ref[hd] = _dot(k_ref[0, pl.ds(off, t_), sl], qt_ref[0, sl, :])

    def softmax(hd):
        mx = s_ref[hd, 0:rc, :]
        for c in range(1, SM_CHUNKS):
            mx = jnp.maximum(mx, s_ref[hd, c * rc:(c + 1) * rc, :])
        m_old = m_ref[hd:hd + 1, :]
        m_new = jnp.maximum(m_old, jnp.max(mx, axis=0, keepdims=True))
        for c in range(SM_CHUNKS):
            rows = slice(c * rc, (c + 1) * rc)
            p_ref[hd, rows, :] = jnp.exp2(s_ref[hd, rows, :] - m_new).astype(BF16)
        al_ref[hd:hd + 1, :] = jnp.exp2(m_old - m_new)
        m_ref[hd:hd + 1, :] = m_new

    def values(off, hd):
        pv = _dot(vt_ref[0, hd, :, pl.ds(off, t_)], p_ref[hd])
        acc_ref[hd] = acc_ref[hd] * al_ref[hd:hd + 1, :] + pv

    def tile_off(j):
        return pl.multiple_of(jnp.clip(j, 0, n_kv - 1) * t_, t_)

    def run_tiles(j0, count):
        items = [(tile_off(j0 + u), hd) for u in range(count) for hd in range(nh)]
        before = (tile_off(j0 - 1), nh - 1)
        after = (tile_off(j0 + count), 0)
        for i, (_, hd) in enumerate(items):
            scores(*(items[i + 1] if i + 1 < len(items) else after))
            values(*(items[i - 1] if i > 0 else before))
            softmax(hd)

    def group_body(g, carry):
        run_tiles(g * ATTN_UNROLL, ATTN_UNROLL)
        return carry

    def single_body(j, carry):
        run_tiles(j, 1)
        return carry

    scores(0, 0)
    n_grp = n_kv // ATTN_UNROLL
    lax.fori_loop(0, n_grp, group_body, 0)
    lax.fori_loop(n_grp * ATTN_UNROLL, n_kv, single_body, 0)
    values(tile_off(n_kv - 1), nh - 1)
    for pr in range(MLA_HEADS // 2):
        ot = jnp.concatenate(
            [acc_ref[hd, 0:MLA_V, :] / acc_ref[hd, MLA_V:MLA_V + 1, :] for hd in (2 * pr, 2 * pr + 1)], axis=0)
        o_ref[0, :, pr * 2 * MLA_V:(pr + 1) * 2 * MLA_V] = ot.T.astype(BF16)


def _attn(qt, k, vt, *, tq, q_start, n_q, n_keys):
    b_ = k.shape[0]
    t_ = TILE
    assert q_start % tq == 0 and n_keys % t_ == 0
    q0 = q_start // tq
    return pl.pallas_call(
        functools.partial(_attn_kernel, n_kv=n_keys // t_),
        out_shape=jax.ShapeDtypeStruct((b_, n_q * tq, MLA_HEADS * MLA_V), BF16),
        grid=(b_, n_q),
        in_specs=[
            pl.BlockSpec((1, MLA_HEADS * HEAD_PAD, tq), lambda b, t: (b, 0, t + q0)),
            pl.BlockSpec((1, n_keys, MLA_HEADS * HEAD_PAD), lambda b, t: (b, 0, 0)),
            pl.BlockSpec((1, MLA_HEADS, V_ROWS, n_keys), lambda b, t: (b, 0, 0, 0)),
        ],
        out_specs=pl.BlockSpec((1, tq, MLA_HEADS * MLA_V), lambda b, t: (b, t, 0)),
        scratch_shapes=[pltpu.VMEM((MLA_HEADS, tq), F32), pltpu.VMEM((MLA_HEADS, tq), F32),
                        pltpu.VMEM((MLA_HEADS, V_ROWS, tq), F32),
                        pltpu.VMEM((MLA_HEADS, t_, tq), F32), pltpu.VMEM((MLA_HEADS, t_, tq), BF16)],
        compiler_params=_params(),
        name="attn",
    )(qt, k, vt)


def _gla_kernel(qkf_ref, gf_ref, vf_ref, qkb_ref, gb_ref, vb_ref, mstf_ref, mstb_ref, of_ref, ob_ref, stf_ref,
                stb_ref):
    @pl.when(pl.program_id(1) == 0)
    def _():
        stf_ref[...] = jnp.zeros_like(stf_ref)
        stb_ref[...] = jnp.zeros_like(stb_ref)

    _gla_direction(qkf_ref, gf_ref, vf_ref, mstf_ref, of_ref, stf_ref, rev=False)
    _gla_direction(qkb_ref, gb_ref, vb_ref, mstb_ref, ob_ref, stb_ref, rev=True)


def _gla_direction(qk_ref, g_ref, v_ref, mst_ref, o_ref, st_ref, *, rev):
    t_ = TILE
    ch = GLA_CHUNK

    q = qk_ref[0, :, 0:GLA_KW] * (GLA_DK ** -0.5)
    k = qk_ref[0, :, GLA_KW:2 * GLA_KW]
    v = v_ref[0].astype(BF16)
    b = _dot_exact_lhs(mst_ref[...], g_ref[0])
    end_row = 0 if rev else ch - 1
    mid_row = ch - 1 - ch // 2 if rev else ch // 2

    def chunk_rows(r):
        return jnp.concatenate(
            [jnp.broadcast_to(b[c * ch + r:c * ch + r + 1], (ch, GLA_KW)) for c in range(t_ // ch)], axis=0)

    bl, mid = chunk_rows(end_row), chunk_rows(mid_row)
    kw = (k * jnp.exp(bl - b)).astype(BF16)
    qe = (q * jnp.exp(b)).astype(BF16)
    q2 = q * jnp.exp(b - mid)
    k2 = (k * jnp.exp(mid - b)).astype(BF16)

    ri = lax.broadcasted_iota(jnp.int32, (t_, t_), 0)
    ci = lax.broadcasted_iota(jnp.int32, (t_, t_), 1)
    same = (ri >> 6) == (ci >> 6)
    tri = (ri <= ci) if rev else (ri >= ci)
    lane_k = lax.broadcasted_iota(jnp.int32, (1, GLA_KW), 1)
    lane_v = lax.broadcasted_iota(jnp.int32, (1, GLA_VW), 1)
    o_intra = jnp.zeros((t_, GLA_VW), F32)
    for hd in range(GLA_HEADS):
        qh = jnp.where((lane_k >> 5) == hd, q2, 0.0).astype(BF16)
        att = _dot_nt(qh, k2)
        att = jnp.where(same, jnp.where(tri, att, 0.0), 0.0).astype(BF16)
        o_intra = o_intra + jnp.where((lane_v >> 6) == hd, _dot(att, v), 0.0)

    rs = lax.broadcasted_iota(jnp.int32, (GLA_VW, GLA_KW), 0)
    cs_ = lax.broadcasted_iota(jnp.int32, (GLA_VW, GLA_KW), 1)
    bd = (rs >> 6) == (cs_ >> 5)
    st = st_ref[...]
    order = range(t_ // ch - 1, -1, -1) if rev else range(t_ // ch)
    for c in order:
        rows = slice(c * ch, (c + 1) * ch)
        o_ref[0, rows, :] = o_intra[rows] + _dot_nt(qe[rows], st.astype(BF16))
        dec = jnp.exp(bl[c * ch:c * ch + 1])
        st = dec * st + jnp.where(bd, _dot_tn(v[rows], kw[rows]), 0.0)
    st_ref[...] = st


def _gla_consts(rev):
    t_, ch = TILE, GLA_CHUNK
    i = np.arange(t_)[:, None]
    j = np.arange(t_)[None, :]
    same = (i // ch) == (j // ch)
    cum = same & ((j >= i) if rev else (j <= i))
    return jnp.asarray(cum.astype(np.float32), dtype=BF16)


def _gla(gqk, gg, gv, *, n_ctx_tiles):
    b_, s_, _ = gqk.shape
    t_ = TILE
    nt = s_ // t_
    fwd = lambda t: t
    bwd = lambda t: jnp.where(t < n_ctx_tiles, n_ctx_tiles - 1 - t, nt - 1 - (t - n_ctx_tiles))
    mstf, mstb = _gla_consts(False), _gla_consts(True)

    def specs(tile, d):
        return [pl.BlockSpec((1, t_, 2 * GLA_KW), lambda b, t: (b, tile(t), 0)),
                pl.BlockSpec((1, t_, GLA_KW), lambda b, t: (b, tile(t), d)),
                pl.BlockSpec((1, t_, GLA_VW), lambda b, t: (b, tile(t), 0))]

    out = jax.ShapeDtypeStruct((b_, s_, GLA_VW), F32)
    return pl.pallas_call(
        _gla_kernel,
        out_shape=[out, out],
        grid=(b_, nt),
        in_specs=specs(fwd, 0) + specs(bwd, 1) + [_const_spec(mstf.shape), _const_spec(mstb.shape)],
        out_specs=[pl.BlockSpec((1, t_, GLA_VW), lambda b, t: (b, fwd(t), 0)),
                   pl.BlockSpec((1, t_, GLA_VW), lambda b, t: (b, bwd(t), 0))],
        scratch_shapes=[pltpu.VMEM((GLA_VW, GLA_KW), F32), pltpu.VMEM((GLA_VW, GLA_KW), F32)],
        compiler_params=_params(),
        name="gla",
    )(gqk, gg, gv, gqk, gg, gv, mstf, mstb)


def _segment(t, n_ctx_tiles, n_tiles):
    in_ctx = t < n_ctx_tiles
    return jnp.where(in_ctx, t, t - n_ctx_tiles), jnp.where(in_ctx, n_ctx_tiles, n_tiles - n_ctx_tiles)


def _halo_specs(width, n_ctx_tiles, n_tiles, t0, nb=1):
    per = TILE // HALO

    def prev(b, t):
        tt = t + t0
        ts, _ = _segment(tt, n_ctx_tiles, n_tiles)
        return (b, jnp.where(ts == 0, tt * per, tt * per - 1), 0)

    def nxt(b, t):
        tt = t + t0
        ts, n = _segment(tt, n_ctx_tiles, n_tiles)
        return (b, jnp.where(ts == n - 1, (tt + 1) * per - 1, (tt + 1) * per), 0)

    return pl.BlockSpec((nb, HALO, width), prev), pl.BlockSpec((nb, HALO, width), nxt)


def _post_kernel(x_ref, mod_ref, pu_ref, pp_ref, pn_ref, attc_ref, attl_ref, of_ref, ob_ref, og_ref, gn_ref, wpool_ref,
                 ps_ref, wo_ref, ones_ref, o_ref, ue_ref, *, n_ctx_tiles, n_tiles, t0):
    t_ = TILE
    tt = pl.program_id(1) + t0
    ts, nseg = _segment(tt, n_ctx_tiles, n_tiles)
    seg_len = nseg * t_

    ue_ref[0:HALO] = jnp.where(ts > 0, pp_ref[0], 0.0)
    ue_ref[HALO:HALO + t_] = pu_ref[0]
    ue_ref[HALO + t_:2 * HALO + t_] = jnp.where(ts < nseg - 1, pn_ref[0], 0.0)
    pos = ts * t_ + lax.broadcasted_iota(jnp.int32, (t_, 1), 0)
    lane = lax.broadcasted_iota(jnp.int32, (1, LANES), 1)

    def shifted(offsets, c0):
        acc = None
        for o in offsets:
            r = ue_ref[pl.ds(HALO + o, t_), c0:c0 + LANES]
            acc = r if acc is None else acc + r
        return acc

    def count(w):
        lo = pos - w // 2
        return (jnp.minimum(lo + w, seg_len) - jnp.maximum(lo, 0)).astype(F32)

    w2 = shifted((-1, 0), 0)
    w4 = w2 + shifted((-2, 1), 0)
    w8 = shifted(range(-4, 4), LANES)
    w16 = w8 + shifted(tuple(range(-8, -4)) + tuple(range(4, 8)), LANES)
    u = pu_ref[0]
    d01 = jnp.where(lane < POOL_GC, w2 / count(2), w4 / count(4)) - u[:, 0:LANES]
    d23 = jnp.where(lane < POOL_GC, w8 / count(8), w16 / count(16)) - u[:, LANES:2 * LANES]
    diff = jnp.concatenate([d01, d23], axis=1).astype(BF16)
    ypool = _dot(diff, wpool_ref[...]) * ps_ref[...]

    o = of_ref[0] + ob_ref[0]
    ms = _dot_exact_rhs(o * o, ones_ref[...]) * (1.0 / GLA_DV)
    yg = o * lax.rsqrt(ms + EPS) * gn_ref[...] * _silu(og_ref[0])

    att = jnp.where(tt < n_ctx_tiles, attc_ref[0], attl_ref[0])
    ycat = jnp.concatenate([ypool.astype(BF16), att, yg.astype(BF16)], axis=1)
    o_ref[0] = x_ref[0] + mod_ref[0][2:3] * _dot(ycat, wo_ref[...])


def _post(xa, mod, pu, att_c, att_l, of, ob, og, gn, wpool, ps, wo, ones, *, n_ctx_tiles, ctx_row, t0):
    b_, s_, d = xa.shape
    t_ = TILE
    nt = s_ // t_
    tok = lambda w: pl.BlockSpec((1, t_, w), lambda b, t: (b, t + t0, 0))
    hp, hn = _halo_specs(POOL_W, n_ctx_tiles, nt, t0)
    return pl.pallas_call(
        functools.partial(_post_kernel, n_ctx_tiles=n_ctx_tiles, n_tiles=nt, t0=t0),
        out_shape=jax.ShapeDtypeStruct((b_, (nt - t0) * t_, d), F32),
        grid=(b_, nt - t0),
        in_specs=[
            tok(d),
            pl.BlockSpec((1, 6, d), lambda b, t: (jnp.where(t + t0 < n_ctx_tiles, ctx_row, b), 0, 0)),
            tok(POOL_W), hp, hn,
            pl.BlockSpec((1, t_, MLA_HEADS * MLA_V), lambda b, t: (b, jnp.minimum(t + t0, max(n_ctx_tiles - 1, 0)), 0)),
            pl.BlockSpec((1, t_, MLA_HEADS * MLA_V), lambda b, t: (b, jnp.maximum(t + t0 - n_ctx_tiles, 0), 0)),
            tok(GLA_VW), tok(GLA_VW), tok(GLA_VW),
            _const_spec(gn.shape), _const_spec(wpool.shape), _const_spec(ps.shape), _const_spec(wo.shape),
            _const_spec(ones.shape),
        ],
        out_specs=pl.BlockSpec((1, t_, d), lambda b, t: (b, t, 0)),
        scratch_shapes=[pltpu.VMEM((t_ + 2 * HALO, POOL_W), F32)],
        compiler_params=_params(),
        name="post",
    )(xa, mod, pu, pu, pu, att_c, att_l, of, ob, og, gn, wpool, ps, wo, ones)


def _ffn_kernel(x_ref, xp_ref, xn_ref, mod_ref, n2_ref, wup_ref, cw_ref, cb_ref, wdn_ref, nf_ref, o_ref, ge_ref,
                *, n_ctx_tiles, n_tiles, final):
    t_ = TILE
    fc = D_FF // FF_CHUNKS
    ts, nseg = _segment(pl.program_id(1), n_ctx_tiles, n_tiles)
    n2 = n2_ref[...]
    row = lax.broadcasted_iota(jnp.int32, (t_ + 2 * HALO, 1), 0)
    row_lo = jnp.where(ts > 0, 0, HALO)
    row_hi = jnp.where(ts < nseg - 1, t_ + 2 * HALO, t_ + HALO)

    nb = x_ref.shape[0]

    def normed(sb):
        mod = mod_ref[sb]
        xe = jnp.concatenate([xp_ref[sb], x_ref[sb], xn_ref[sb]], axis=0)
        hf = _rms(xe, n2) * (1.0 + mod[4:5]) + mod[3:4]
        return hf.astype(BF16), hf[HALO:HALO + t_].astype(BF16)

    def up(sb, c, he, h):
        cg = slice(D_FF + c * fc, D_FF + (c + 1) * fc)
        u = _dot(h, wup_ref[:, c * fc:(c + 1) * fc])
        g_ext = _dot(he, wup_ref[:, cg])
        ge_ref[sb] = jnp.where(row >= row_lo, jnp.where(row < row_hi, g_ext, 0.0), 0.0)
        return u, g_ext[HALO:HALO + t_]

    def down(sb, c, u, g):
        cu = slice(c * fc, (c + 1) * fc)
        cw = cw_ref[:, cu]
        gc = (ge_ref[sb, pl.ds(HALO - 1, t_), :] * cw[0:1] + g * cw[1:2]
              + ge_ref[sb, pl.ds(HALO + 1, t_), :] * cw[2:3] + cb_ref[:, cu])
        return _dot((_silu(gc) * u).astype(BF16), wdn_ref[cu, :])

    hs = [None] * nb
    ugs = [None] * nb
    ys = [0.0] * nb
    hs[0] = normed(0)
    for c in range(FF_CHUNKS):
        for sb in range(nb):
            ugs[sb] = up(sb, c, *hs[sb])
            if c == 0 and sb + 1 < nb:
                hs[sb + 1] = normed(sb + 1)
            if sb > 0:
                ys[sb - 1] = ys[sb - 1] + down(sb - 1, c, *ugs[sb - 1])
        ys[nb - 1] = ys[nb - 1] + down(nb - 1, c, *ugs[nb - 1])
    for sb in range(nb):
        x2 = x_ref[sb] + mod_ref[sb][5:6] * ys[sb]
        o_ref[sb] = _rms(x2, nf_ref[...]) if final else x2


def _ffn(x1, mod, n2, wup, cw, cb, wdn, nf, *, n_ctx_tiles, ctx_row, final):
    b_, s_, d = x1.shape
    t_ = TILE
    nt = s_ // t_
    nb = FF_BATCH if b_ % FF_BATCH == 0 and ctx_row % FF_BATCH == 0 else 1
    hp, hn = _halo_specs(d, n_ctx_tiles, nt, 0, nb)
    return pl.pallas_call(
        functools.partial(_ffn_kernel, n_ctx_tiles=n_ctx_tiles, n_tiles=nt, final=final),
        out_shape=jax.ShapeDtypeStruct((b_, s_, d), F32),
        grid=(b_ // nb, nt),
        in_specs=[
            pl.BlockSpec((nb, t_, d), lambda b, t: (b, t, 0)), hp, hn,
            pl.BlockSpec((nb, 6, d), lambda b, t: (jnp.where(t < n_ctx_tiles, ctx_row // nb, b), 0, 0)),
            _const_spec(n2.shape), _const_spec(wup.shape), _const_spec(cw.shape), _const_spec(cb.shape),
            _const_spec(wdn.shape), _const_spec(nf.shape),
        ],
        out_specs=pl.BlockSpec((nb, t_, d), lambda b, t: (b, t, 0)),
        scratch_shapes=[pltpu.VMEM((nb, t_ + 2 * HALO, D_FF // FF_CHUNKS), F32)],
        compiler_params=_params(),
        name="ffn",
    )(x1, x1, x1, mod, n2, wup, cw, cb, wdn, nf)


def _rot_cols(w):
    w4 = w.reshape(w.shape[:-1] + (2, 2, MLA_ROPE // 4))
    return jnp.stack([-w4[..., 1, :], w4[..., 0, :]], axis=-2).reshape(w.shape)


def _prep_weights(w_in, w_uq, w_ukv, w_gk_f, b_gk_f, w_gk_b, b_gk_b, w_pool, gla_norm):
    depth, d, _ = w_in.shape
    z = lambda *s: jnp.zeros((depth,) + s, F32)
    o = 0
    parts = {}
    for name, size in (("ckv", KV_LORA), ("kr", MLA_ROPE), ("gk", GLA_KW), ("gv", GLA_VW),
                       ("lrf", GLA_GATE_RANK), ("lrb", GLA_GATE_RANK), ("pool", POOL_W), ("cq", Q_LORA),
                       ("gq", GLA_KW), ("og", GLA_VW)):
        parts[name] = w_in[..., o:o + size]
        o += size
    win = jnp.concatenate([
        parts["ckv"], parts["cq"], parts["pool"], parts["gq"], parts["gk"], parts["gv"], parts["og"],
        parts["lrf"], parts["lrb"], z(d, 32), parts["kr"], z(d, 32),
        z(d, 64), _rot_cols(parts["kr"]), z(d, 32)], axis=-1).astype(BF16)

    uq = w_uq.reshape(depth, Q_LORA, MLA_HEADS, MLA_QK)
    nope, rope = uq[..., :MLA_NOPE], uq[..., MLA_NOPE:]
    zq = lambda w: jnp.zeros((depth, Q_LORA, MLA_HEADS, w), F32)
    qa = jnp.concatenate([nope, rope, zq(32)], axis=-1).reshape(depth, Q_LORA, -1)
    qb = jnp.concatenate([zq(64), _rot_cols(rope), zq(32)], axis=-1).reshape(depth, Q_LORA, -1)
    wuqt = jnp.swapaxes(jnp.concatenate([qa, qb], axis=-1), 1, 2).astype(BF16)

    ukv = w_ukv.reshape(depth, KV_LORA, MLA_HEADS, MLA_NOPE + MLA_V)
    kn = jnp.concatenate([ukv[..., :MLA_NOPE], jnp.zeros((depth, KV_LORA, MLA_HEADS, 64), F32)], axis=-1)
    wuk = kn.reshape(depth, KV_LORA, -1).astype(BF16)
    wvt = jnp.swapaxes(ukv[..., MLA_NOPE:].reshape(depth, KV_LORA, -1), 1, 2).astype(BF16)

    r = GLA_GATE_RANK
    wg = jnp.zeros((depth, LANES, 2 * GLA_KW), F32)
    wg = wg.at[:, 0:r, 0:GLA_KW].set(w_gk_f).at[:, r:2 * r, GLA_KW:].set(w_gk_b).astype(BF16)
    bg = jnp.concatenate([b_gk_f, b_gk_b], axis=-1)[:, None, :]

    wpool = jnp.zeros((depth, POOL_W, POOL_W), F32)
    for g in range(POOL_GROUPS):
        sl = slice(g * POOL_GC, (g + 1) * POOL_GC)
        wpool = wpool.at[:, sl, sl].set(w_pool[:, g])
    gn = jnp.tile(gla_norm, (1, GLA_HEADS))[:, None, :]
    return win, wuqt, wuk, wvt, wg, bg, wpool.astype(BF16), gn


def _rope_tables(seq, ctx_len):
    rows = seq // GRID_W
    row = jnp.repeat(jnp.arange(rows), GRID_W).astype(F32)
    col = jnp.tile(jnp.arange(GRID_W), rows).astype(F32)
    half = MLA_ROPE // 2
    inv = ROPE_THETA ** (-jnp.arange(0, half, 2, dtype=F32) / half)
    ar = row[:, None] * inv
    ac = col[:, None] * inv
    ang = jnp.concatenate([ar, ar, ac, ac], axis=-1)
    cos = jnp.concatenate([jnp.ones((ctx_len, MLA_ROPE), F32), jnp.cos(ang)], axis=0)
    sin = jnp.concatenate([jnp.zeros((ctx_len, MLA_ROPE), F32), jnp.sin(ang)], axis=0)
    s_ = ctx_len + seq
    scale = MLA_QK ** -0.5 * LOG2E
    z = lambda w: jnp.zeros((s_, w), F32)
    tabk = jnp.concatenate([z(MLA_NOPE), cos, z(32), z(MLA_NOPE), sin, z(32)], axis=-1)
    tabq = jnp.concatenate([jnp.ones((s_, MLA_NOPE), F32) * scale, cos * scale, z(32),
                            z(MLA_NOPE), sin * scale, z(32)], axis=-1)
    return tabk, tabq.T


def kernel(x, c, ctx, c_ctx, w_ada, b_ada, norm1, norm2, w_in, q_norm, w_uq, kv_norm, w_ukv, w_gk_f, b_gk_f,
           w_gk_b, b_gk_b, gla_norm, w_pool, pool_scale, w_o, w_up, conv_w, conv_b, w_down, norm_f):
    b_, seq, d = x.shape
    ctx_len = ctx.shape[1]
    depth = w_ada.shape[0]
    assert seq % TILE == 0 and ctx_len % TILE == 0 and seq % GRID_W == 0
    nc = ctx_len // TILE

    bp = -(-(b_ + FF_BATCH) // 8) * 8
    cvec = jnp.concatenate([c, jnp.tile(c_ctx[None, :], (FF_BATCH, 1)), jnp.zeros((bp - b_ - FF_BATCH, d), F32)],
                           axis=0)
    mod_all = _ada(cvec, w_ada, b_ada).reshape(depth, bp, 6, d)

    win, wuqt, wuk, wvt, wg, bg, wpool, gn = _prep_weights(w_in, w_uq, w_ukv, w_gk_f, b_gk_f, w_gk_b, b_gk_b,
                                                      w_pool, gla_norm)
    wo = w_o.astype(BF16)
    wup = w_up.astype(BF16)
    wdn = w_down.astype(BF16)
    tabk, tabq = _rope_tables(seq, ctx_len)
    gi = np.arange(GLA_VW) // GLA_DV
    ones = jnp.asarray((gi[:, None] == gi[None, :]).astype(np.float32), dtype=BF16)

    xa = jnp.concatenate([ctx, x], axis=1)
    tq = ATTN_TQ if seq % ATTN_TQ == 0 else TILE
    q_pad = -ctx_len % tq
    for i in range(depth):
        last = i == depth - 1
        t0 = nc if last else 0
        mod = mod_all[i]
        q, k, v, pu, gqk, gg, gv, og = _pre(xa, mod, norm1[i][None], win[i], q_norm[i][None], wuqt[i],
                                            kv_norm[i][None], wuk[i], wvt[i], wg[i], bg[i], tabk, tabq,
                                            n_ctx_tiles=nc, ctx_row=b_, q_pad=q_pad)
        att_l = _attn(q, k, v, tq=tq, q_start=q_pad + ctx_len, n_q=seq // tq, n_keys=ctx_len + seq)
        att_c = att_l if last else _attn(q, k, v, tq=TILE, q_start=q_pad, n_q=nc, n_keys=ctx_len)
        of, ob = _gla(gqk, gg, gv, n_ctx_tiles=nc)
        x1 = _post(xa, mod, pu, att_c, att_l, of, ob, og, gn[i], wpool[i], pool_scale[i][None], wo[i], ones,
                   n_ctx_tiles=nc, ctx_row=b_, t0=t0)
        xa = _ffn(x1, mod, norm2[i][None], wup[i], conv_w[i], conv_b[i][None], wdn[i], norm_f[None],
                  n_ctx_tiles=0 if last else nc, ctx_row=b_, final=last)
    return xa
```

```python
import functools

import numpy as np
import jax
import jax.numpy as jnp
from jax import lax
from jax.experimental import pallas as pl
from jax.experimental.pallas import tpu as pltpu

F32 = jnp.float32
BF16 = jnp.bfloat16

EPS = 1e-6
LOG2E = 1.4426950408889634
GRID_W = 64
POOL_W = 256
POOL_GROUPS = 4
POOL_GC = POOL_W // POOL_GROUPS
POOL_WINDOWS = (2, 4, 8, 16)
MLA_HEADS = 8
MLA_NOPE = 64
MLA_ROPE = 32
MLA_V = 64
MLA_QK = MLA_NOPE + MLA_ROPE
Q_LORA = 256
KV_LORA = 128
ROPE_THETA = 10000.0
GLA_HEADS = 4
GLA_DK = 32
GLA_DV = 64
GLA_GATE_RANK = 16
GLA_GATE_NORM = 16.0
GLA_CHUNK = 64
GLA_KW = GLA_HEADS * GLA_DK
GLA_VW = GLA_HEADS * GLA_DV
D_FF = 2816

TILE = 256
HALO = 8
LANES = 128
HEAD_PAD = LANES
V_ROWS = 80
MIX_W = POOL_W + MLA_HEADS * MLA_V + GLA_VW
FF_CHUNKS = 1
FF_BATCH = 2
SM_CHUNKS = 4
ATTN_UNROLL = 4
ATTN_TQ = 512
VMEM_LIMIT = 56 * 1024 * 1024

C_CKV = 0
C_CQ = C_CKV + KV_LORA
C_POOL = C_CQ + Q_LORA
C_GQK = C_POOL + POOL_W
C_GV = C_GQK + 2 * GLA_KW
C_OG = C_GV + GLA_VW
C_MISC_A = C_OG + GLA_VW
C_MISC_B = C_MISC_A + LANES
IN_COLS_P = C_MISC_B + LANES


def _dot(a, b):
    return jnp.dot(a, b, preferred_element_type=F32)


def _dot_nt(a, b):
    return lax.dot_general(a, b, (((1,), (1,)), ((), ())), preferred_element_type=F32)


def _dot_tn(a, b):
    return lax.dot_general(a, b, (((0,), (0,)), ((), ())), preferred_element_type=F32)


def _dot_exact_rhs(a, m):
    a1 = a.astype(BF16)
    r = a - a1.astype(F32)
    a2 = r.astype(BF16)
    a3 = (r - a2.astype(F32)).astype(BF16)
    return _dot(a1, m) + _dot(a2, m) + _dot(a3, m)


def _dot_exact_lhs(m, a):
    a1 = a.astype(BF16)
    r = a - a1.astype(F32)
    a2 = r.astype(BF16)
    a3 = (r - a2.astype(F32)).astype(BF16)
    return _dot(m, a1) + _dot(m, a2) + _dot(m, a3)


def _rms(x, g):
    return x * lax.rsqrt(jnp.mean(x * x, axis=-1, keepdims=True) + EPS) * g


def _silu(x):
    return x * jax.nn.sigmoid(x)


def _params():
    return pltpu.CompilerParams(vmem_limit_bytes=VMEM_LIMIT)


def _const_spec(shape):
    nd = len(shape)
    return pl.BlockSpec(shape, lambda *_: (0,) * nd, pipeline_mode=pl.Buffered(1))


def _ada_kernel(c_ref, w_ref, b_ref, o_ref):
    s = _silu(c_ref[...]).astype(BF16)
    o_ref[0] = _dot(s, w_ref[0].astype(BF16)) + b_ref[0]


def _ada(cvec, w_ada, b_ada):
    depth, d, n = w_ada.shape
    bp = cvec.shape[0]
    tn = n // 4
    return pl.pallas_call(
        _ada_kernel,
        out_shape=jax.ShapeDtypeStruct((depth, bp, n), F32),
        grid=(depth, n // tn),
        in_specs=[
            pl.BlockSpec((bp, d), lambda i, j: (0, 0)),
            pl.BlockSpec((1, d, tn), lambda i, j: (i, 0, j)),
            pl.BlockSpec((1, 1, tn), lambda i, j: (i, 0, j)),
        ],
        out_specs=pl.BlockSpec((1, bp, tn), lambda i, j: (i, 0, j)),
        compiler_params=_params(),
        name="ada",
    )(cvec, w_ada, b_ada.reshape(depth, 1, n))


def _pre_kernel(x_ref, mod_ref, n1_ref, win_ref, qn_ref, wuqt_ref, kvn_ref, wuk_ref, wvt_ref, wg_ref, bg_ref, tabk_ref,
                tabq_ref, qt_ref, k_ref, vt_ref, pu_ref, gqk_ref, gg_ref, gv_ref, og_ref):
    x = x_ref[0]
    mod = mod_ref[0]
    h = _rms(x, n1_ref[...]) * (1.0 + mod[1:2]) + mod[0:1]
    z = _dot(h.astype(BF16), win_ref[...])
    pu_ref[0] = z[:, C_POOL:C_POOL + POOL_W]
    gqk_ref[0] = z[:, C_GQK:C_GQK + 2 * GLA_KW]
    gv_ref[0] = z[:, C_GV:C_GV + GLA_VW]
    og_ref[0] = z[:, C_OG:C_OG + GLA_VW]
    za = z[:, C_MISC_A:C_MISC_A + LANES]
    zb = z[:, C_MISC_B:C_MISC_B + LANES]
    cosk, sink = tabk_ref[:, 0:LANES], tabk_ref[:, LANES:2 * LANES]

    a = _dot(za.astype(BF16), wg_ref[...]) + bg_ref[...]
    gg_ref[0] = (jnp.minimum(a, 0.0) - jnp.log1p(jnp.exp(-jnp.abs(a)))) * (1.0 / GLA_GATE_NORM)

    ckvn = _rms(z[:, C_CKV:C_CKV + KV_LORA], kvn_ref[...]).astype(BF16)
    uk = _dot(ckvn, wuk_ref[...])
    kr = za * cosk + zb * sink
    kw = MLA_HEADS * HEAD_PAD
    for hd in range(MLA_HEADS):
        sl = slice(hd * HEAD_PAD, (hd + 1) * HEAD_PAD)
        k_ref[0, :, sl] = (uk[:, sl] + kr).astype(BF16)
    vt = _dot_nt(wvt_ref[...], ckvn)
    pad_row = lax.broadcasted_iota(jnp.int32, (V_ROWS - MLA_V, x.shape[0]), 0)
    pad = jnp.where(pad_row == 0, 1.0, 0.0).astype(BF16)
    for hd in range(MLA_HEADS):
        vt_ref[0, hd, 0:MLA_V, :] = vt[hd * MLA_V:(hd + 1) * MLA_V].astype(BF16)
        vt_ref[0, hd, MLA_V:V_ROWS, :] = pad

    cqn = _rms(z[:, C_CQ:C_CQ + Q_LORA], qn_ref[...]).astype(BF16)
    uqt = _dot_nt(wuqt_ref[...], cqn)
    cosq, sinq = tabq_ref[0:LANES, :], tabq_ref[LANES:2 * LANES, :]
    for hd in range(MLA_HEADS):
        sl = slice(hd * HEAD_PAD, (hd + 1) * HEAD_PAD)
        sr = slice(kw + hd * HEAD_PAD, kw + (hd + 1) * HEAD_PAD)
        qt_ref[0, sl, :] = (uqt[sl] * cosq + uqt[sr] * sinq).astype(BF16)


def _pre(xa, mod, n1, win, qn, wuqt, kvn, wuk, wvt, wg, bg, tabk, tabq, *, n_ctx_tiles, ctx_row):
    b_, s_, d = xa.shape
    t_ = TILE
    nt = s_ // t_
    tok = lambda w: pl.BlockSpec((1, t_, w), lambda b, t: (b, t, 0))
    outs = [(MLA_HEADS * HEAD_PAD, BF16), (POOL_W, F32), (2 * GLA_KW, F32), (2 * GLA_KW, F32), (GLA_VW, F32),
            (GLA_VW, F32)]
    qt_shape = jax.ShapeDtypeStruct((b_, MLA_HEADS * HEAD_PAD, s_), BF16)
    qt_spec = pl.BlockSpec((1, MLA_HEADS * HEAD_PAD, t_),
                           lambda b, t: (b, 0, jnp.where(t < n_ctx_tiles, nt - n_ctx_tiles + t, t - n_ctx_tiles)))
    vt_shape = jax.ShapeDtypeStruct((b_, MLA_HEADS, V_ROWS, s_), BF16)
    vt_spec = pl.BlockSpec((1, MLA_HEADS, V_ROWS, t_), lambda b, t: (b, 0, 0, t))
    tok_shapes = [jax.ShapeDtypeStruct((b_, s_, w), dt) for w, dt in outs]
    tok_specs = [tok(w) for w, _ in outs]
    return pl.pallas_call(
        _pre_kernel,
        out_shape=[qt_shape, tok_shapes[0], vt_shape] + tok_shapes[1:],
        grid=(b_, nt),
        in_specs=[
            tok(d),
            pl.BlockSpec((1, 6, d), lambda b, t: (jnp.where(t < n_ctx_tiles, ctx_row, b), 0, 0)),
            _const_spec(n1.shape), _const_spec(win.shape), _const_spec(qn.shape), _const_spec(wuqt.shape),
            _const_spec(kvn.shape), _const_spec(wuk.shape), _const_spec(wvt.shape), _const_spec(wg.shape),
            _const_spec(bg.shape),
            pl.BlockSpec((t_, 2 * LANES), lambda b, t: (t, 0)),
            pl.BlockSpec((2 * LANES, t_), lambda b, t: (0, t)),
        ],
        out_specs=[qt_spec, tok_specs[0], vt_spec] + tok_specs[1:],
        compiler_params=_params(),
        name="pre",
    )(xa, mod, n1, win, qn, wuqt, kvn, wuk, wvt, wg, bg, tabk, tabq)


def _attn_kernel(qt_ref, k_ref, vt_ref, o_ref, m_ref, al_ref, acc_ref, s_ref, p_ref, *, n_kv):
    t_ = TILE
    nh = MLA_HEADS
    rc = t_ // SM_CHUNKS
    m_ref[...] = jnp.full(m_ref.shape, -jnp.inf, F32)
    al_ref[...] = jnp.ones(al_ref.shape, F32)
    acc_ref[...] = jnp.zeros(acc_ref.shape, F32)
    p_ref[...] = jnp.zeros(p_ref.shape, BF16)

    def scores(off, hd):
        sl = slice(hd * HEAD_PAD, (hd + 1) * HEAD_PAD)
        s_ref[hd] = _dot(k_ref[0, pl.ds(off, t_), sl], qt_ref[0, sl, :])

    def softmax(hd):
        mx = s_ref[hd, 0:rc, :]
        for c in range(1, SM_CHUNKS):
            mx = jnp.maximum(mx, s_ref[hd, c * rc:(c + 1) * rc, :])
        m_old = m_ref[hd:hd + 1, :]
        m_new = jnp.maximum(m_old, jnp.max(mx, axis=0, keepdims=True))
        for c in range(SM_CHUNKS):
            rows = slice(c * rc, (c + 1) * rc)
            p_ref[hd, rows, :] = jnp.exp2(s_ref[hd, rows, :] - m_new).astype(BF16)
        al_ref[hd:hd + 1, :] = jnp.exp2(m_old - m_new)
        m_ref[hd:hd + 1, :] = m_new

    def values(off, hd):
        pv = _dot(vt_ref[0, hd, :, pl.ds(off, t_)], p_ref[hd])
        acc_ref[hd] = acc_ref[hd] * al_ref[hd:hd + 1, :] + pv

    def tile_off(j):
        return pl.multiple_of(jnp.clip(j, 0, n_kv - 1) * t_, t_)

    def run_tiles(j0, count):
        items = [(tile_off(j0 + u), hd) for u in range(count) for hd in range(nh)]
        before = (tile_off(j0 - 1), nh - 1)
        after = (tile_off(j0 + count), 0)
        for i, (_, hd) in enumerate(items):
            scores(*(items[i + 1] if i + 1 < len(items) else after))
            values(*(items[i - 1] if i > 0 else before))
            softmax(hd)

    def group_body(g, carry):
        run_tiles(g * ATTN_UNROLL, ATTN_UNROLL)
        return carry

    def single_body(j, carry):
        run_tiles(j, 1)
        return carry

    scores(0, 0)
    n_grp = n_kv // ATTN_UNROLL
    lax.fori_loop(0, n_grp, group_body, 0)
    lax.fori_loop(n_grp * ATTN_UNROLL, n_kv, single_body, 0)
    values(tile_off(n_kv - 1), nh - 1)
    for pr in range(MLA_HEADS // 2):
        ot = jnp.concatenate(
            [acc_ref[hd, 0:MLA_V, :] / acc_ref[hd, MLA_V:MLA_V + 1, :] for hd in (2 * pr, 2 * pr + 1)], axis=0)
        o_ref[0, :, pr * 2 * MLA_V:(pr + 1) * 2 * MLA_V] = ot.T.astype(BF16)


def _attn(qt, k, vt, *, tq, q_start, n_q, n_keys):
    b_ = k.shape[0]
    t_ = TILE
    assert q_start % tq == 0 and n_keys % t_ == 0
    q0 = q_start // tq
    return pl.pallas_call(
        functools.partial(_attn_kernel, n_kv=n_keys // t_),
        out_shape=jax.ShapeDtypeStruct((b_, n_q * tq, MLA_HEADS * MLA_V), BF16),
        grid=(b_, n_q),
        in_specs=[
            pl.BlockSpec((1, MLA_HEADS * HEAD_PAD, tq), lambda b, t: (b, 0, t + q0)),
            pl.BlockSpec((1, n_keys, MLA_HEADS * HEAD_PAD), lambda b, t: (b, 0, 0)),
            pl.BlockSpec((1, MLA_HEADS, V_ROWS, n_keys), lambda b, t: (b, 0, 0, 0)),
        ],
        out_specs=pl.BlockSpec((1, tq, MLA_HEADS * MLA_V), lambda b, t: (b, t, 0)),
        scratch_shapes=[pltpu.VMEM((MLA_HEADS, tq), F32), pltpu.VMEM((MLA_HEADS, tq), F32),
                        pltpu.VMEM((MLA_HEADS, V_ROWS, tq), F32),
                        pltpu.VMEM((MLA_HEADS, t_, tq), F32), pltpu.VMEM((MLA_HEADS, t_, tq), BF16)],
        compiler_params=_params(),
        name="attn",
    )(qt, k, vt)


def _gla_kernel(qkf_ref, gf_ref, vf_ref, qkb_ref, gb_ref, vb_ref, mstf_ref, mstb_ref, of_ref, ob_ref, stf_ref,
                stb_ref):
    @pl.when(pl.program_id(1) == 0)
    def _():
        stf_ref[...] = jnp.zeros_like(stf_ref)
        stb_ref[...] = jnp.zeros_like(stb_ref)

    chains = [_gla_direction(qkf_ref, gf_ref, vf_ref, mstf_ref, of_ref, stf_ref, rev=False),
              _gla_direction(qkb_ref, gb_ref, vb_ref, mstb_ref, ob_ref, stb_ref, rev=True)]
    while chains:
        chains = [ch for ch in chains if next(ch, "done") != "done"]


def _gla_direction(qk_ref, g_ref, v_ref, mst_ref, o_ref, st_ref, *, rev):
    t_ = TILE
    ch = GLA_CHUNK

    q = qk_ref[0, :, 0:GLA_KW] * (GLA_DK ** -0.5)
    k = qk_ref[0, :, GLA_KW:2 * GLA_KW]
    v = v_ref[0].astype(BF16)
    b = _dot_exact_lhs(mst_ref[...], g_ref[0])
    end_row = 0 if rev else ch - 1
    mid_row = ch - 1 - ch // 2 if rev else ch // 2

    def chunk_rows(r):
        return jnp.concatenate(
            [jnp.broadcast_to(b[c * ch + r:c * ch + r + 1], (ch, GLA_KW)) for c in range(t_ // ch)], axis=0)

    bl, mid = chunk_rows(end_row), chunk_rows(mid_row)
    kw = (k * jnp.exp(bl - b)).astype(BF16)
    qe = (q * jnp.exp(b)).astype(BF16)
    q2 = q * jnp.exp(b - mid)
    k2 = (k * jnp.exp(mid - b)).astype(BF16)
    yield

    ri = lax.broadcasted_iota(jnp.int32, (t_, t_), 0)
    ci = lax.broadcasted_iota(jnp.int32, (t_, t_), 1)
    same = (ri >> 6) == (ci >> 6)
    tri = (ri <= ci) if rev else (ri >= ci)
    lane_k = lax.broadcasted_iota(jnp.int32, (1, GLA_KW), 1)
    lane_v = lax.broadcasted_iota(jnp.int32, (1, GLA_VW), 1)
    o_intra = jnp.zeros((t_, GLA_VW), F32)
    for hd in range(GLA_HEADS):
        qh = jnp.where((lane_k >> 5) == hd, q2, 0.0).astype(BF16)
        att = _dot_nt(qh, k2)
        att = jnp.where(same, jnp.where(tri, att, 0.0), 0.0).astype(BF16)
        o_intra = o_intra + jnp.where((lane_v >> 6) == hd, _dot(att, v), 0.0)
        yield

    rs = lax.broadcasted_iota(jnp.int32, (GLA_VW, GLA_KW), 0)
    cs_ = lax.broadcasted_iota(jnp.int32, (GLA_VW, GLA_KW), 1)
    bd = (rs >> 6) == (cs_ >> 5)
    st = st_ref[...]
    order = range(t_ // ch - 1, -1, -1) if rev else range(t_ // ch)
    for c in order:
        rows = slice(c * ch, (c + 1) * ch)
        o_ref[0, rows, :] = o_intra[rows] + _dot_nt(qe[rows], st.astype(BF16))
        dec = jnp.exp(bl[c * ch:c * ch + 1])
        st = dec * st + jnp.where(bd, _dot_tn(v[rows], kw[rows]), 0.0)
        yield
    st_ref[...] = st


def _gla_consts(rev):
    t_, ch = TILE, GLA_CHUNK
    i = np.arange(t_)[:, None]
    j = np.arange(t_)[None, :]
    same = (i // ch) == (j // ch)
    cum = same & ((j >= i) if rev else (j <= i))
    return jnp.asarray(cum.astype(np.float32), dtype=BF16)


def _gla(gqk, gg, gv, *, n_ctx_tiles):
    b_, s_, _ = gqk.shape
    t_ = TILE
    nt = s_ // t_
    fwd = lambda t: t
    bwd = lambda t: jnp.where(t < n_ctx_tiles, n_ctx_tiles - 1 - t, nt - 1 - (t - n_ctx_tiles))
    mstf, mstb = _gla_consts(False), _gla_consts(True)

    def specs(tile, d):
        return [pl.BlockSpec((1, t_, 2 * GLA_KW), lambda b, t: (b, tile(t), 0)),
                pl.BlockSpec((1, t_, GLA_KW), lambda b, t: (b, tile(t), d)),
                pl.BlockSpec((1, t_, GLA_VW), lambda b, t: (b, tile(t), 0))]

    out = jax.ShapeDtypeStruct((b_, s_, GLA_VW), F32)
    return pl.pallas_call(
        _gla_kernel,
        out_shape=[out, out],
        grid=(b_, nt),
        in_specs=specs(fwd, 0) + specs(bwd, 1) + [_const_spec(mstf.shape), _const_spec(mstb.shape)],
        out_specs=[pl.BlockSpec((1, t_, GLA_VW), lambda b, t: (b, fwd(t), 0)),
                   pl.BlockSpec((1, t_, GLA_VW), lambda b, t: (b, bwd(t), 0))],
        scratch_shapes=[pltpu.VMEM((GLA_VW, GLA_KW), F32), pltpu.VMEM((GLA_VW, GLA_KW), F32)],
        compiler_params=_params(),
        name="gla",
    )(gqk, gg, gv, gqk, gg, gv, mstf, mstb)


def _segment(t, n_ctx_tiles, n_tiles):
    in_ctx = t < n_ctx_tiles
    return jnp.where(in_ctx, t, t - n_ctx_tiles), jnp.where(in_ctx, n_ctx_tiles, n_tiles - n_ctx_tiles)


def _halo_specs(width, n_ctx_tiles, n_tiles, t0, nb=1):
    per = TILE // HALO

    def prev(b, t):
        tt = t + t0
        ts, _ = _segment(tt, n_ctx_tiles, n_tiles)
        return (b, jnp.where(ts == 0, tt * per, tt * per - 1), 0)

    def nxt(b, t):
        tt = t + t0
        ts, n = _segment(tt, n_ctx_tiles, n_tiles)
        return (b, jnp.where(ts == n - 1, (tt + 1) * per - 1, (tt + 1) * per), 0)

    return pl.BlockSpec((nb, HALO, width), prev), pl.BlockSpec((nb, HALO, width), nxt)


def _post_kernel(x_ref, mod_ref, pu_ref, pp_ref, pn_ref, attc_ref, attl_ref, of_ref, ob_ref, og_ref, gn_ref, wpool_ref,
                 ps_ref, wo_ref, ones_ref, o_ref, ue_ref, *, n_ctx_tiles, n_tiles, t0):
    t_ = TILE
    tt = pl.program_id(1) + t0
    ts, nseg = _segment(tt, n_ctx_tiles, n_tiles)
    seg_len = nseg * t_

    ue_ref[0:HALO] = jnp.where(ts > 0, pp_ref[0], 0.0)
    ue_ref[HALO:HALO + t_] = pu_ref[0]
    ue_ref[HALO + t_:2 * HALO + t_] = jnp.where(ts < nseg - 1, pn_ref[0], 0.0)
    pos = ts * t_ + lax.broadcasted_iota(jnp.int32, (t_, 1), 0)
    lane = lax.broadcasted_iota(jnp.int32, (1, LANES), 1)

    def shifted(offsets, c0):
        acc = None
        for o in offsets:
            r = ue_ref[pl.ds(HALO + o, t_), c0:c0 + LANES]
            acc = r if acc is None else acc + r
        return acc

    def count(w):
        lo = pos - w // 2
        return (jnp.minimum(lo + w, seg_len) - jnp.maximum(lo, 0)).astype(F32)

    w2 = shifted((-1, 0), 0)
    w4 = w2 + shifted((-2, 1), 0)
    w8 = shifted(range(-4, 4), LANES)
    w16 = w8 + shifted(tuple(range(-8, -4)) + tuple(range(4, 8)), LANES)
    u = pu_ref[0]
    d01 = jnp.where(lane < POOL_GC, w2 / count(2), w4 / count(4)) - u[:, 0:LANES]
    d23 = jnp.where(lane < POOL_GC, w8 / count(8), w16 / count(16)) - u[:, LANES:2 * LANES]
    diff = jnp.concatenate([d01, d23], axis=1).astype(BF16)
    ypool = _dot(diff, wpool_ref[...]) * ps_ref[...]

    o = of_ref[0] + ob_ref[0]
    ms = _dot_exact_rhs(o * o, ones_ref[...]) * (1.0 / GLA_DV)
    yg = o * lax.rsqrt(ms + EPS) * gn_ref[...] * _silu(og_ref[0])

    att = jnp.where(tt < n_ctx_tiles, attc_ref[0], attl_ref[0])
    ycat = jnp.concatenate([ypool.astype(BF16), att, yg.astype(BF16)], axis=1)
    o_ref[0] = x_ref[0] + mod_ref[0][2:3] * _dot(ycat, wo_ref[...])


def _post(xa, mod, pu, att_c, att_l, of, ob, og, gn, wpool, ps, wo, ones, *, n_ctx_tiles, ctx_row, t0):
    b_, s_, d = xa.shape
    t_ = TILE
    nt = s_ // t_
    tok = lambda w: pl.BlockSpec((1, t_, w), lambda b, t: (b, t + t0, 0))
    hp, hn = _halo_specs(POOL_W, n_ctx_tiles, nt, t0)
    return pl.pallas_call(
        functools.partial(_post_kernel, n_ctx_tiles=n_ctx_tiles, n_tiles=nt, t0=t0),
        out_shape=jax.ShapeDtypeStruct((b_, (nt - t0) * t_, d), F32),
        grid=(b_, nt - t0),
        in_specs=[
            tok(d),
            pl.BlockSpec((1, 6, d), lambda b, t: (jnp.where(t + t0 < n_ctx_tiles, ctx_row, b), 0, 0)),
            tok(POOL_W), hp, hn,
            pl.BlockSpec((1, t_, MLA_HEADS * MLA_V), lambda b, t: (b, jnp.minimum(t + t0, max(n_ctx_tiles - 1, 0)), 0)),
            pl.BlockSpec((1, t_, MLA_HEADS * MLA_V), lambda b, t: (b, jnp.maximum(t + t0 - n_ctx_tiles, 0), 0)),
            tok(GLA_VW), tok(GLA_VW), tok(GLA_VW),
            _const_spec(gn.shape), _const_spec(wpool.shape), _const_spec(ps.shape), _const_spec(wo.shape),
            _const_spec(ones.shape),
        ],
        out_specs=pl.BlockSpec((1, t_, d), lambda b, t: (b, t, 0)),
        scratch_shapes=[pltpu.VMEM((t_ + 2 * HALO, POOL_W), F32)],
        compiler_params=_params(),
        name="post",
    )(xa, mod, pu, pu, pu, att_c, att_l, of, ob, og, gn, wpool, ps, wo, ones)


def _ffn_kernel(x_ref, xp_ref, xn_ref, mod_ref, n2_ref, wup_ref, cw_ref, cb_ref, wdn_ref, nf_ref, o_ref, ge_ref,
                *, n_ctx_tiles, n_tiles, final):
    t_ = TILE
    fc = D_FF // FF_CHUNKS
    ts, nseg = _segment(pl.program_id(1), n_ctx_tiles, n_tiles)
    n2 = n2_ref[...]
    row = lax.broadcasted_iota(jnp.int32, (t_ + 2 * HALO, 1), 0)
    row_lo = jnp.where(ts > 0, 0, HALO)
    row_hi = jnp.where(ts < nseg - 1, t_ + 2 * HALO, t_ + HALO)

    nb = x_ref.shape[0]

    def normed(sb):
        mod = mod_ref[sb]
        xe = jnp.concatenate([xp_ref[sb], x_ref[sb], xn_ref[sb]], axis=0)
        hf = _rms(xe, n2) * (1.0 + mod[4:5]) + mod[3:4]
        return hf.astype(BF16), hf[HALO:HALO + t_].astype(BF16)

    def up(sb, c, he, h):
        cg = slice(D_FF + c * fc, D_FF + (c + 1) * fc)
        u = _dot(h, wup_ref[:, c * fc:(c + 1) * fc])
        g_ext = _dot(he, wup_ref[:, cg])
        ge_ref[sb] = jnp.where(row >= row_lo, jnp.where(row < row_hi, g_ext, 0.0), 0.0)
        return u, g_ext[HALO:HALO + t_]

    def down(sb, c, u, g):
        cu = slice(c * fc, (c + 1) * fc)
        cw = cw_ref[:, cu]
        gc = (ge_ref[sb, pl.ds(HALO - 1, t_), :] * cw[0:1] + g * cw[1:2]
              + ge_ref[sb, pl.ds(HALO + 1, t_), :] * cw[2:3] + cb_ref[:, cu])
        return _dot((_silu(gc) * u).astype(BF16), wdn_ref[cu, :])

    hs = [None] * nb
    ugs = [None] * nb
    ys = [0.0] * nb
    hs[0] = normed(0)
    for c in range(FF_CHUNKS):
        for sb in range(nb):
            ugs[sb] = up(sb, c, *hs[sb])
            if c == 0 and sb + 1 < nb:
                hs[sb + 1] = normed(sb + 1)
            if sb > 0:
                ys[sb - 1] = ys[sb - 1] + down(sb - 1, c, *ugs[sb - 1])
        ys[nb - 1] = ys[nb - 1] + down(nb - 1, c, *ugs[nb - 1])
    for sb in range(nb):
        x2 = x_ref[sb] + mod_ref[sb][5:6] * ys[sb]
        o_ref[sb] = _rms(x2, nf_ref[...]) if final else x2


def _ffn(x1, mod, n2, wup, cw, cb, wdn, nf, *, n_ctx_tiles, ctx_row, final):
    b_, s_, d = x1.shape
    t_ = TILE
    nt = s_ // t_
    nb = FF_BATCH if b_ % FF_BATCH == 0 and ctx_row % FF_BATCH == 0 else 1
    hp, hn = _halo_specs(d, n_ctx_tiles, nt, 0, nb)
    return pl.pallas_call(
        functools.partial(_ffn_kernel, n_ctx_tiles=n_ctx_tiles, n_tiles=nt, final=final),
        out_shape=jax.ShapeDtypeStruct((b_, s_, d), F32),
        grid=(b_ // nb, nt),
        in_specs=[
            pl.BlockSpec((nb, t_, d), lambda b, t: (b, t, 0)), hp, hn,
            pl.BlockSpec((nb, 6, d), lambda b, t: (jnp.where(t < n_ctx_tiles, ctx_row // nb, b), 0, 0)),
            _const_spec(n2.shape), _const_spec(wup.shape), _const_spec(cw.shape), _const_spec(cb.shape),
            _const_spec(wdn.shape), _const_spec(nf.shape),
        ],
        out_specs=pl.BlockSpec((nb, t_, d), lambda b, t: (b, t, 0)),
        scratch_shapes=[pltpu.VMEM((nb, t_ + 2 * HALO, D_FF // FF_CHUNKS), F32)],
        compiler_params=_params(),
        name="ffn",
    )(x1, x1, x1, mod, n2, wup, cw, cb, wdn, nf)


def _rot_cols(w):
    w4 = w.reshape(w.shape[:-1] + (2, 2, MLA_ROPE // 4))
    return jnp.stack([-w4[..., 1, :], w4[..., 0, :]], axis=-2).reshape(w.shape)


def _prep_weights(w_in, w_uq, w_ukv, w_gk_f, b_gk_f, w_gk_b, b_gk_b, w_pool, gla_norm):
    depth, d, _ = w_in.shape
    z = lambda *s: jnp.zeros((depth,) + s, F32)
    o = 0
    parts = {}
    for name, size in (("ckv", KV_LORA), ("kr", MLA_ROPE), ("gk", GLA_KW), ("gv", GLA_VW),
                       ("lrf", GLA_GATE_RANK), ("lrb", GLA_GATE_RANK), ("pool", POOL_W), ("cq", Q_LORA),
                       ("gq", GLA_KW), ("og", GLA_VW)):
        parts[name] = w_in[..., o:o + size]
        o += size
    win = jnp.concatenate([
        parts["ckv"], parts["cq"], parts["pool"], parts["gq"], parts["gk"], parts["gv"], parts["og"],
        parts["lrf"], parts["lrb"], z(d, 32), parts["kr"], z(d, 32),
        z(d, 64), _rot_cols(parts["kr"]), z(d, 32)], axis=-1).astype(BF16)

    uq = w_uq.reshape(depth, Q_LORA, MLA_HEADS, MLA_QK)
    nope, rope = uq[..., :MLA_NOPE], uq[..., MLA_NOPE:]
    zq = lambda w: jnp.zeros((depth, Q_LORA, MLA_HEADS, w), F32)
    qa = jnp.concatenate([nope, rope, zq(32)], axis=-1).reshape(depth, Q_LORA, -1)
    qb = jnp.concatenate([zq(64), _rot_cols(rope), zq(32)], axis=-1).reshape(depth, Q_LORA, -1)
    wuqt = jnp.swapaxes(jnp.concatenate([qa, qb], axis=-1), 1, 2).astype(BF16)

    ukv = w_ukv.reshape(depth, KV_LORA, MLA_HEADS, MLA_NOPE + MLA_V)
    kn = jnp.concatenate([ukv[..., :MLA_NOPE], jnp.zeros((depth, KV_LORA, MLA_HEADS, 64), F32)], axis=-1)
    wuk = kn.reshape(depth, KV_LORA, -1).astype(BF16)
    wvt = jnp.swapaxes(ukv[..., MLA_NOPE:].reshape(depth, KV_LORA, -1), 1, 2).astype(BF16)

    r = GLA_GATE_RANK
    wg = jnp.zeros((depth, LANES, 2 * GLA_KW), F32)
    wg = wg.at[:, 0:r, 0:GLA_KW].set(w_gk_f).at[:, r:2 * r, GLA_KW:].set(w_gk_b).astype(BF16)
    bg = jnp.concatenate([b_gk_f, b_gk_b], axis=-1)[:, None, :]

    wpool = jnp.zeros((depth, POOL_W, POOL_W), F32)
    for g in range(POOL_GROUPS):
        sl = slice(g * POOL_GC, (g + 1) * POOL_GC)
        wpool = wpool.at[:, sl, sl].set(w_pool[:, g])
    gn = jnp.tile(gla_norm, (1, GLA_HEADS))[:, None, :]
    return win, wuqt, wuk, wvt, wg, bg, wpool.astype(BF16), gn


def _rope_tables(seq, ctx_len):
    rows = seq // GRID_W
    row = jnp.repeat(jnp.arange(rows), GRID_W).astype(F32)
    col = jnp.tile(jnp.arange(GRID_W), rows).astype(F32)
    half = MLA_ROPE // 2
    inv = ROPE_THETA ** (-jnp.arange(0, half, 2, dtype=F32) / half)
    ar = row[:, None] * inv
    ac = col[:, None] * inv
    ang = jnp.concatenate([ar, ar, ac, ac], axis=-1)
    cos = jnp.concatenate([jnp.ones((ctx_len, MLA_ROPE), F32), jnp.cos(ang)], axis=0)
    sin = jnp.concatenate([jnp.zeros((ctx_len, MLA_ROPE), F32), jnp.sin(ang)], axis=0)
    s_ = ctx_len + seq
    scale = MLA_QK ** -0.5 * LOG2E
    z = lambda w: jnp.zeros((s_, w), F32)
    tabk = jnp.concatenate([z(MLA_NOPE), cos, z(32), z(MLA_NOPE), sin, z(32)], axis=-1)
    tabq = jnp.concatenate([jnp.ones((s_, MLA_NOPE), F32) * scale, cos * scale, z(32),
                            z(MLA_NOPE), sin * scale, z(32)], axis=-1)
    return tabk, tabq.T


def kernel(x, c, ctx, c_ctx, w_ada, b_ada, norm1, norm2, w_in, q_norm, w_uq, kv_norm, w_ukv, w_gk_f, b_gk_f,
           w_gk_b, b_gk_b, gla_norm, w_pool, pool_scale, w_o, w_up, conv_w, conv_b, w_down, norm_f):
    b_, seq, d = x.shape
    ctx_len = ctx.shape[1]
    depth = w_ada.shape[0]
    assert seq % TILE == 0 and ctx_len % TILE == 0 and seq % GRID_W == 0
    nc = ctx_len // TILE

    bp = -(-(b_ + FF_BATCH) // 8) * 8
    cvec = jnp.concatenate([c, jnp.tile(c_ctx[None, :], (FF_BATCH, 1)), jnp.zeros((bp - b_ - FF_BATCH, d), F32)],
                           axis=0)
    mod_all = _ada(cvec, w_ada, b_ada).reshape(depth, bp, 6, d)

    win, wuqt, wuk, wvt, wg, bg, wpool, gn = _prep_weights(w_in, w_uq, w_ukv, w_gk_f, b_gk_f, w_gk_b, b_gk_b,
                                                      w_pool, gla_norm)
    wo = w_o.astype(BF16)
    wup = w_up.astype(BF16)
    wdn = w_down.astype(BF16)
    tabk, tabq = _rope_tables(seq, ctx_len)
    gi = np.arange(GLA_VW) // GLA_DV
    ones = jnp.asarray((gi[:, None] == gi[None, :]).astype(np.float32), dtype=BF16)

    xa = jnp.concatenate([ctx, x], axis=1)
    tq = ATTN_TQ if seq % ATTN_TQ == 0 else TILE
    for i in range(depth):
        last = i == depth - 1
        t0 = nc if last else 0
        mod = mod_all[i]
        q, k, v, pu, gqk, gg, gv, og = _pre(xa, mod, norm1[i][None], win[i], q_norm[i][None], wuqt[i],
                                            kv_norm[i][None], wuk[i], wvt[i], wg[i], bg[i], tabk, tabq,
                                            n_ctx_tiles=nc, ctx_row=b_)
        att_l = _attn(q, k, v, tq=tq, q_start=0, n_q=seq // tq, n_keys=ctx_len + seq)
        att_c = att_l if last else _attn(q, k, v, tq=TILE, q_start=seq, n_q=nc, n_keys=ctx_len)
        of, ob = _gla(gqk, gg, gv, n_ctx_tiles=nc)
        x1 = _post(xa, mod, pu, att_c, att_l, of, ob, og, gn[i], wpool[i], pool_scale[i][None], wo[i], ones,
                   n_ctx_tiles=nc, ctx_row=b_, t0=t0)
        xa = _ffn(x1, mod, norm2[i][None], wup[i], conv_w[i], conv_b[i][None], wdn[i], norm_f[None],
                  n_ctx_tiles=0 if last else nc, ctx_row=b_, final=last)
    return xa
```

```python
import functools

import numpy as np
import jax
import jax.numpy as jnp
from jax import lax
from jax.experimental import pallas as pl
from jax.experimental.pallas import tpu as pltpu

F32 = jnp.float32
BF16 = jnp.bfloat16

EPS = 1e-6
LOG2E = 1.4426950408889634
GRID_W = 64
POOL_W = 256
POOL_GROUPS = 4
POOL_GC = POOL_W // POOL_GROUPS
POOL_WINDOWS = (2, 4, 8, 16)
MLA_HEADS = 8
MLA_NOPE = 64
MLA_ROPE = 32
MLA_V = 64
MLA_QK = MLA_NOPE + MLA_ROPE
Q_LORA = 256
KV_LORA = 128
ROPE_THETA = 10000.0
GLA_HEADS = 4
GLA_DK = 32
GLA_DV = 64
GLA_GATE_RANK = 16
GLA_GATE_NORM = 16.0
GLA_CHUNK = 64
GLA_KW = GLA_HEADS * GLA_DK
GLA_VW = GLA_HEADS * GLA_DV
D_FF = 2816

TILE = 256
HALO = 8
LANES = 128
HEAD_PAD = LANES
V_ROWS = 80
MIX_W = POOL_W + MLA_HEADS * MLA_V + GLA_VW
FF_CHUNKS = 1
FF_BATCH = 2
GLA_BATCH = 2
SM_CHUNKS = 4
ATTN_UNROLL = 4
ATTN_TQ = 512
VMEM_LIMIT = 56 * 1024 * 1024

C_CKV = 0
C_CQ = C_CKV + KV_LORA
C_POOL = C_CQ + Q_LORA
C_GQK = C_POOL + POOL_W
C_GV = C_GQK + 2 * GLA_KW
C_OG = C_GV + GLA_VW
C_MISC_A = C_OG + GLA_VW
C_MISC_B = C_MISC_A + LANES
IN_COLS_P = C_MISC_B + LANES


def _dot(a, b):
    return jnp.dot(a, b, preferred_element_type=F32)


def _dot_nt(a, b):
    return lax.dot_general(a, b, (((1,), (1,)), ((), ())), preferred_element_type=F32)


def _dot_tn(a, b):
    return lax.dot_general(a, b, (((0,), (0,)), ((), ())), preferred_element_type=F32)


def _dot_exact_rhs(a, m):
    a1 = a.astype(BF16)
    r = a - a1.astype(F32)
    a2 = r.astype(BF16)
    a3 = (r - a2.astype(F32)).astype(BF16)
    return _dot(a1, m) + _dot(a2, m) + _dot(a3, m)


def _dot_exact_lhs(m, a):
    a1 = a.astype(BF16)
    r = a - a1.astype(F32)
    a2 = r.astype(BF16)
    a3 = (r - a2.astype(F32)).astype(BF16)
    return _dot(m, a1) + _dot(m, a2) + _dot(m, a3)


def _rms(x, g):
    return x * lax.rsqrt(jnp.mean(x * x, axis=-1, keepdims=True) + EPS) * g


def _silu(x):
    return x * jax.nn.sigmoid(x)


def _params():
    return pltpu.CompilerParams(vmem_limit_bytes=VMEM_LIMIT)


def _const_spec(shape):
    nd = len(shape)
    return pl.BlockSpec(shape, lambda *_: (0,) * nd, pipeline_mode=pl.Buffered(1))


def _ada_kernel(c_ref, w_ref, b_ref, o_ref):
    s = _silu(c_ref[...]).astype(BF16)
    o_ref[0] = _dot(s, w_ref[0].astype(BF16)) + b_ref[0]


def _ada(cvec, w_ada, b_ada):
    depth, d, n = w_ada.shape
    bp = cvec.shape[0]
    tn = n // 4
    return pl.pallas_call(
        _ada_kernel,
        out_shape=jax.ShapeDtypeStruct((depth, bp, n), F32),
        grid=(depth, n // tn),
        in_specs=[
            pl.BlockSpec((bp, d), lambda i, j: (0, 0)),
            pl.BlockSpec((1, d, tn), lambda i, j: (i, 0, j)),
            pl.BlockSpec((1, 1, tn), lambda i, j: (i, 0, j)),
        ],
        out_specs=pl.BlockSpec((1, bp, tn), lambda i, j: (i, 0, j)),
        compiler_params=_params(),
        name="ada",
    )(cvec, w_ada, b_ada.reshape(depth, 1, n))


def _pre_kernel(x_ref, mod_ref, n1_ref, win_ref, qn_ref, wuqt_ref, kvn_ref, wuk_ref, wvt_ref, wg_ref, bg_ref, tabk_ref,
                tabq_ref, qt_ref, k_ref, vt_ref, pu_ref, gqk_ref, gg_ref, gv_ref, og_ref):
    x = x_ref[0]
    mod = mod_ref[0]
    h = _rms(x, n1_ref[...]) * (1.0 + mod[1:2]) + mod[0:1]
    z = _dot(h.astype(BF16), win_ref[...])
    pu_ref[0] = z[:, C_POOL:C_POOL + POOL_W]
    gqk_ref[0] = z[:, C_GQK:C_GQK + 2 * GLA_KW]
    gv_ref[0] = z[:, C_GV:C_GV + GLA_VW]
    og_ref[0] = z[:, C_OG:C_OG + GLA_VW]
    za = z[:, C_MISC_A:C_MISC_A + LANES]
    zb = z[:, C_MISC_B:C_MISC_B + LANES]
    cosk, sink = tabk_ref[:, 0:LANES], tabk_ref[:, LANES:2 * LANES]

    a = _dot(za.astype(BF16), wg_ref[...]) + bg_ref[...]
    gg_ref[0] = (jnp.minimum(a, 0.0) - jnp.log1p(jnp.exp(-jnp.abs(a)))) * (1.0 / GLA_GATE_NORM)

    ckvn = _rms(z[:, C_CKV:C_CKV + KV_LORA], kvn_ref[...]).astype(BF16)
    uk = _dot(ckvn, wuk_ref[...])
    kr = za * cosk + zb * sink
    kw = MLA_HEADS * HEAD_PAD
    for hd in range(MLA_HEADS):
        sl = slice(hd * HEAD_PAD, (hd + 1) * HEAD_PAD)
        k_ref[0, :, sl] = (uk[:, sl] + kr).astype(BF16)
    vt = _dot_nt(wvt_ref[...], ckvn)
    pad_row = lax.broadcasted_iota(jnp.int32, (V_ROWS - MLA_V, x.shape[0]), 0)
    pad = jnp.where(pad_row == 0, 1.0, 0.0).astype(BF16)
    for hd in range(MLA_HEADS):
        vt_ref[0, hd, 0:MLA_V, :] = vt[hd * MLA_V:(hd + 1) * MLA_V].astype(BF16)
        vt_ref[0, hd, MLA_V:V_ROWS, :] = pad

    cqn = _rms(z[:, C_CQ:C_CQ + Q_LORA], qn_ref[...]).astype(BF16)
    uqt = _dot_nt(wuqt_ref[...], cqn)
    cosq, sinq = tabq_ref[0:LANES, :], tabq_ref[LANES:2 * LANES, :]
    for hd in range(MLA_HEADS):
        sl = slice(hd * HEAD_PAD, (hd + 1) * HEAD_PAD)
        sr = slice(kw + hd * HEAD_PAD, kw + (hd + 1) * HEAD_PAD)
        qt_ref[0, sl, :] = (uqt[sl] * cosq + uqt[sr] * sinq).astype(BF16)


def _pre(xa, mod, n1, win, qn, wuqt, kvn, wuk, wvt, wg, bg, tabk, tabq, *, n_ctx_tiles, ctx_row):
    b_, s_, d = xa.shape
    t_ = TILE
    nt = s_ // t_
    tok = lambda w: pl.BlockSpec((1, t_, w), lambda b, t: (b, t, 0))
    outs = [(MLA_HEADS * HEAD_PAD, BF16), (POOL_W, F32), (2 * GLA_KW, F32), (2 * GLA_KW, F32), (GLA_VW, F32),
            (GLA_VW, F32)]
    qt_shape = jax.ShapeDtypeStruct((b_, MLA_HEADS * HEAD_PAD, s_), BF16)
    qt_spec = pl.BlockSpec((1, MLA_HEADS * HEAD_PAD, t_),
                           lambda b, t: (b, 0, jnp.where(t < n_ctx_tiles, nt - n_ctx_tiles + t, t - n_ctx_tiles)))
    vt_shape = jax.ShapeDtypeStruct((b_, MLA_HEADS, V_ROWS, s_), BF16)
    vt_spec = pl.BlockSpec((1, MLA_HEADS, V_ROWS, t_), lambda b, t: (b, 0, 0, t))
    tok_shapes = [jax.ShapeDtypeStruct((b_, s_, w), dt) for w, dt in outs]
    tok_specs = [tok(w) for w, _ in outs]
    return pl.pallas_call(
        _pre_kernel,
        out_shape=[qt_shape, tok_shapes[0], vt_shape] + tok_shapes[1:],
        grid=(b_, nt),
        in_specs=[
            tok(d),
            pl.BlockSpec((1, 6, d), lambda b, t: (jnp.where(t < n_ctx_tiles, ctx_row, b), 0, 0)),
            _const_spec(n1.shape), _const_spec(win.shape), _const_spec(qn.shape), _const_spec(wuqt.shape),
            _const_spec(kvn.shape), _const_spec(wuk.shape), _const_spec(wvt.shape), _const_spec(wg.shape),
            _const_spec(bg.shape),
            pl.BlockSpec((t_, 2 * LANES), lambda b, t: (t, 0)),
            pl.BlockSpec((2 * LANES, t_), lambda b, t: (0, t)),
        ],
        out_specs=[qt_spec, tok_specs[0], vt_spec] + tok_specs[1:],
        compiler_params=_params(),
        name="pre",
    )(xa, mod, n1, win, qn, wuqt, kvn, wuk, wvt, wg, bg, tabk, tabq)


def _attn_kernel(qt_ref, k_ref, vt_ref, o_ref, m_ref, al_ref, acc_ref, s_ref, p_ref, *, n_kv):
    t_ = TILE
    nh = MLA_HEADS
    rc = t_ // SM_CHUNKS
    m_ref[...] = jnp.full(m_ref.shape, -jnp.inf, F32)
    al_ref[...] = jnp.ones(al_ref.shape, F32)
    acc_ref[...] = jnp.zeros(acc_ref.shape, F32)
    p_ref[...] = jnp.zeros(p_ref.shape, BF16)

    def scores(off, hd):
        sl = slice(hd * HEAD_PAD, (hd + 1) * HEAD_PAD)
        s_ref[hd] = _dot(k_ref[0, pl.ds(off, t_), sl], qt_ref[0, sl, :])

    def softmax(hd):
        mx = s_ref[hd, 0:rc, :]
        for c in range(1, SM_CHUNKS):
            mx = jnp.maximum(mx, s_ref[hd, c * rc:(c + 1) * rc, :])
        m_old = m_ref[hd:hd + 1, :]
        m_new = jnp.maximum(m_old, jnp.max(mx, axis=0, keepdims=True))
        for c in range(SM_CHUNKS):
            rows = slice(c * rc, (c + 1) * rc)
            p_ref[hd, rows, :] = jnp.exp2(s_ref[hd, rows, :] - m_new).astype(BF16)
        al_ref[hd:hd + 1, :] = jnp.exp2(m_old - m_new)
        m_ref[hd:hd + 1, :] = m_new

    def values(off, hd):
        pv = _dot(vt_ref[0, hd, :, pl.ds(off, t_)], p_ref[hd])
        acc_ref[hd] = acc_ref[hd] * al_ref[hd:hd + 1, :] + pv

    def tile_off(j):
        return pl.multiple_of(jnp.clip(j, 0, n_kv - 1) * t_, t_)

    def run_tiles(j0, count):
        items = [(tile_off(j0 + u), hd) for u in range(count) for hd in range(nh)]
        before = (tile_off(j0 - 1), nh - 1)
        after = (tile_off(j0 + count), 0)
        for i, (_, hd) in enumerate(items):
            scores(*(items[i + 1] if i + 1 < len(items) else after))
            values(*(items[i - 1] if i > 0 else before))
            softmax(hd)

    def group_body(g, carry):
        run_tiles(g * ATTN_UNROLL, ATTN_UNROLL)
        return carry

    def single_body(j, carry):
        run_tiles(j, 1)
        return carry

    scores(0, 0)
    n_grp = n_kv // ATTN_UNROLL
    lax.fori_loop(0, n_grp, group_body, 0)
    lax.fori_loop(n_grp * ATTN_UNROLL, n_kv, single_body, 0)
    values(tile_off(n_kv - 1), nh - 1)
    for pr in range(MLA_HEADS // 2):
        ot = jnp.concatenate(
            [acc_ref[hd, 0:MLA_V, :] / acc_ref[hd, MLA_V:MLA_V + 1, :] for hd in (2 * pr, 2 * pr + 1)], axis=0)
        o_ref[0, :, pr * 2 * MLA_V:(pr + 1) * 2 * MLA_V] = ot.T.astype(BF16)


def _attn(qt, k, vt, *, tq, q_start, n_q, n_keys):
    b_ = k.shape[0]
    t_ = TILE
    assert q_start % tq == 0 and n_keys % t_ == 0
    q0 = q_start // tq
    return pl.pallas_call(
        functools.partial(_attn_kernel, n_kv=n_keys // t_),
        out_shape=jax.ShapeDtypeStruct((b_, n_q * tq, MLA_HEADS * MLA_V), BF16),
        grid=(b_, n_q),
        in_specs=[
            pl.BlockSpec((1, MLA_HEADS * HEAD_PAD, tq), lambda b, t: (b, 0, t + q0)),
            pl.BlockSpec((1, n_keys, MLA_HEADS * HEAD_PAD), lambda b, t: (b, 0, 0)),
            pl.BlockSpec((1, MLA_HEADS, V_ROWS, n_keys), lambda b, t: (b, 0, 0, 0)),
        ],
        out_specs=pl.BlockSpec((1, tq, MLA_HEADS * MLA_V), lambda b, t: (b, t, 0)),
        scratch_shapes=[pltpu.VMEM((MLA_HEADS, tq), F32), pltpu.VMEM((MLA_HEADS, tq), F32),
                        pltpu.VMEM((MLA_HEADS, V_ROWS, tq), F32),
                        pltpu.VMEM((MLA_HEADS, t_, tq), F32), pltpu.VMEM((MLA_HEADS, t_, tq), BF16)],
        compiler_params=_params(),
        name="attn",
    )(qt, k, vt)


def _gla_kernel(qkf_ref, gf_ref, vf_ref, qkb_ref, gb_ref, vb_ref, mstf_ref, mstb_ref, of_ref, ob_ref, stf_ref,
                stb_ref):
    @pl.when(pl.program_id(1) == 0)
    def _():
        stf_ref[...] = jnp.zeros_like(stf_ref)
        stb_ref[...] = jnp.zeros_like(stb_ref)

    chains = []
    for sb in range(qkf_ref.shape[0]):
        chains.append(_gla_direction(qkf_ref, gf_ref, vf_ref, mstf_ref, of_ref, stf_ref, sb, rev=False))
        chains.append(_gla_direction(qkb_ref, gb_ref, vb_ref, mstb_ref, ob_ref, stb_ref, sb, rev=True))
    while chains:
        chains = [ch for ch in chains if next(ch, "done") != "done"]


def _gla_direction(qk_ref, g_ref, v_ref, mst_ref, o_ref, st_ref, sb, *, rev):
    t_ = TILE
    ch = GLA_CHUNK

    q = qk_ref[sb, :, 0:GLA_KW] * (GLA_DK ** -0.5)
    k = qk_ref[sb, :, GLA_KW:2 * GLA_KW]
    v = v_ref[sb].astype(BF16)
    b = _dot_exact_lhs(mst_ref[...], g_ref[sb])
    end_row = 0 if rev else ch - 1
    mid_row = ch - 1 - ch // 2 if rev else ch // 2

    def chunk_rows(r):
        return jnp.concatenate(
            [jnp.broadcast_to(b[c * ch + r:c * ch + r + 1], (ch, GLA_KW)) for c in range(t_ // ch)], axis=0)

    bl, mid = chunk_rows(end_row), chunk_rows(mid_row)
    kw = (k * jnp.exp(bl - b)).astype(BF16)
    qe = (q * jnp.exp(b)).astype(BF16)
    q2 = q * jnp.exp(b - mid)
    k2 = (k * jnp.exp(mid - b)).astype(BF16)
    yield

    ri = lax.broadcasted_iota(jnp.int32, (t_, t_), 0)
    ci = lax.broadcasted_iota(jnp.int32, (t_, t_), 1)
    same = (ri >> 6) == (ci >> 6)
    tri = (ri <= ci) if rev else (ri >= ci)
    lane_k = lax.broadcasted_iota(jnp.int32, (1, GLA_KW), 1)
    lane_v = lax.broadcasted_iota(jnp.int32, (1, GLA_VW), 1)
    o_intra = jnp.zeros((t_, GLA_VW), F32)
    for hd in range(GLA_HEADS):
        qh = jnp.where((lane_k >> 5) == hd, q2, 0.0).astype(BF16)
        att = _dot_nt(qh, k2)
        att = jnp.where(same, jnp.where(tri, att, 0.0), 0.0).astype(BF16)
        o_intra = o_intra + jnp.where((lane_v >> 6) == hd, _dot(att, v), 0.0)
        yield

    rs = lax.broadcasted_iota(jnp.int32, (GLA_VW, GLA_KW), 0)
    cs_ = lax.broadcasted_iota(jnp.int32, (GLA_VW, GLA_KW), 1)
    bd = (rs >> 6) == (cs_ >> 5)
    st = st_ref[sb]
    order = range(t_ // ch - 1, -1, -1) if rev else range(t_ // ch)
    for c in order:
        rows = slice(c * ch, (c + 1) * ch)
        o_ref[sb, rows, :] = o_intra[rows] + _dot_nt(qe[rows], st.astype(BF16))
        dec = jnp.exp(bl[c * ch:c * ch + 1])
        st = dec * st + jnp.where(bd, _dot_tn(v[rows], kw[rows]), 0.0)
        yield
    st_ref[sb] = st


def _gla_consts(rev):
    t_, ch = TILE, GLA_CHUNK
    i = np.arange(t_)[:, None]
    j = np.arange(t_)[None, :]
    same = (i // ch) == (j // ch)
    cum = same & ((j >= i) if rev else (j <= i))
    return jnp.asarray(cum.astype(np.float32), dtype=BF16)


def _gla(gqk, gg, gv, *, n_ctx_tiles):
    b_, s_, _ = gqk.shape
    t_ = TILE
    nt = s_ // t_
    fwd = lambda t: t
    bwd = lambda t: jnp.where(t < n_ctx_tiles, n_ctx_tiles - 1 - t, nt - 1 - (t - n_ctx_tiles))
    mstf, mstb = _gla_consts(False), _gla_consts(True)

    nb = GLA_BATCH if b_ % GLA_BATCH == 0 else 1

    def specs(tile, d):
        return [pl.BlockSpec((nb, t_, 2 * GLA_KW), lambda b, t: (b, tile(t), 0)),
                pl.BlockSpec((nb, t_, GLA_KW), lambda b, t: (b, tile(t), d)),
                pl.BlockSpec((nb, t_, GLA_VW), lambda b, t: (b, tile(t), 0))]

    out = jax.ShapeDtypeStruct((b_, s_, GLA_VW), F32)
    return pl.pallas_call(
        _gla_kernel,
        out_shape=[out, out],
        grid=(b_ // nb, nt),
        in_specs=specs(fwd, 0) + specs(bwd, 1) + [_const_spec(mstf.shape), _const_spec(mstb.shape)],
        out_specs=[pl.BlockSpec((nb, t_, GLA_VW), lambda b, t: (b, fwd(t), 0)),
                   pl.BlockSpec((nb, t_, GLA_VW), lambda b, t: (b, bwd(t), 0))],
        scratch_shapes=[pltpu.VMEM((nb, GLA_VW, GLA_KW), F32), pltpu.VMEM((nb, GLA_VW, GLA_KW), F32)],
        compiler_params=_params(),
        name="gla",
    )(gqk, gg, gv, gqk, gg, gv, mstf, mstb)


def _segment(t, n_ctx_tiles, n_tiles):
    in_ctx = t < n_ctx_tiles
    return jnp.where(in_ctx, t, t - n_ctx_tiles), jnp.where(in_ctx, n_ctx_tiles, n_tiles - n_ctx_tiles)


def _halo_specs(width, n_ctx_tiles, n_tiles, t0, nb=1):
    per = TILE // HALO

    def prev(b, t):
        tt = t + t0
        ts, _ = _segment(tt, n_ctx_tiles, n_tiles)
        return (b, jnp.where(ts == 0, tt * per, tt * per - 1), 0)

    def nxt(b, t):
        tt = t + t0
        ts, n = _segment(tt, n_ctx_tiles, n_tiles)
        return (b, jnp.where(ts == n - 1, (tt + 1) * per - 1, (tt + 1) * per), 0)

    return pl.BlockSpec((nb, HALO, width), prev), pl.BlockSpec((nb, HALO, width), nxt)


def _post_kernel(x_ref, mod_ref, pu_ref, pp_ref, pn_ref, attc_ref, attl_ref, of_ref, ob_ref, og_ref, gn_ref, wpool_ref,
                 ps_ref, wo_ref, ones_ref, o_ref, ue_ref, *, n_ctx_tiles, n_tiles, t0):
    t_ = TILE
    tt = pl.program_id(1) + t0
    ts, nseg = _segment(tt, n_ctx_tiles, n_tiles)
    seg_len = nseg * t_

    ue_ref[0:HALO] = jnp.where(ts > 0, pp_ref[0], 0.0)
    ue_ref[HALO:HALO + t_] = pu_ref[0]
    ue_ref[HALO + t_:2 * HALO + t_] = jnp.where(ts < nseg - 1, pn_ref[0], 0.0)
    pos = ts * t_ + lax.broadcasted_iota(jnp.int32, (t_, 1), 0)
    lane = lax.broadcasted_iota(jnp.int32, (1, LANES), 1)

    def shifted(offsets, c0):
        acc = None
        for o in offsets:
            r = ue_ref[pl.ds(HALO + o, t_), c0:c0 + LANES]
            acc = r if acc is None else acc + r
        return acc

    def count(w):
        lo = pos - w // 2
        return (jnp.minimum(lo + w, seg_len) - jnp.maximum(lo, 0)).astype(F32)

    w2 = shifted((-1, 0), 0)
    w4 = w2 + shifted((-2, 1), 0)
    w8 = shifted(range(-4, 4), LANES)
    w16 = w8 + shifted(tuple(range(-8, -4)) + tuple(range(4, 8)), LANES)
    u = pu_ref[0]
    d01 = jnp.where(lane < POOL_GC, w2 / count(2), w4 / count(4)) - u[:, 0:LANES]
    d23 = jnp.where(lane < POOL_GC, w8 / count(8), w16 / count(16)) - u[:, LANES:2 * LANES]
    diff = jnp.concatenate([d01, d23], axis=1).astype(BF16)
    ypool = _dot(diff, wpool_ref[...]) * ps_ref[...]

    o = of_ref[0] + ob_ref[0]
    ms = _dot_exact_rhs(o * o, ones_ref[...]) * (1.0 / GLA_DV)
    yg = o * lax.rsqrt(ms + EPS) * gn_ref[...] * _silu(og_ref[0])

    att = jnp.where(tt < n_ctx_tiles, attc_ref[0], attl_ref[0])
    ycat = jnp.concatenate([ypool.astype(BF16), att, yg.astype(BF16)], axis=1)
    o_ref[0] = x_ref[0] + mod_ref[0][2:3] * _dot(ycat, wo_ref[...])


def _post(xa, mod, pu, att_c, att_l, of, ob, og, gn, wpool, ps, wo, ones, *, n_ctx_tiles, ctx_row, t0):
    b_, s_, d = xa.shape
    t_ = TILE
    nt = s_ // t_
    tok = lambda w: pl.BlockSpec((1, t_, w), lambda b, t: (b, t + t0, 0))
    hp, hn = _halo_specs(POOL_W, n_ctx_tiles, nt, t0)
    return pl.pallas_call(
        functools.partial(_post_kernel, n_ctx_tiles=n_ctx_tiles, n_tiles=nt, t0=t0),
        out_shape=jax.ShapeDtypeStruct((b_, (nt - t0) * t_, d), F32),
        grid=(b_, nt - t0),
        in_specs=[
            tok(d),
            pl.BlockSpec((1, 6, d), lambda b, t: (jnp.where(t + t0 < n_ctx_tiles, ctx_row, b), 0, 0)),
            tok(POOL_W), hp, hn,
            pl.BlockSpec((1, t_, MLA_HEADS * MLA_V), lambda b, t: (b, jnp.minimum(t + t0, max(n_ctx_tiles - 1, 0)), 0)),
            pl.BlockSpec((1, t_, MLA_HEADS * MLA_V), lambda b, t: (b, jnp.maximum(t + t0 - n_ctx_tiles, 0), 0)),
            tok(GLA_VW), tok(GLA_VW), tok(GLA_VW),
            _const_spec(gn.shape), _const_spec(wpool.shape), _const_spec(ps.shape), _const_spec(wo.shape),
            _const_spec(ones.shape),
        ],
        out_specs=pl.BlockSpec((1, t_, d), lambda b, t: (b, t, 0)),
        scratch_shapes=[pltpu.VMEM((t_ + 2 * HALO, POOL_W), F32)],
        compiler_params=_params(),
        name="post",
    )(xa, mod, pu, pu, pu, att_c, att_l, of, ob, og, gn, wpool, ps, wo, ones)


def _ffn_kernel(x_ref, xp_ref, xn_ref, mod_ref, n2_ref, wup_ref, cw_ref, cb_ref, wdn_ref, nf_ref, o_ref, ge_ref,
                *, n_ctx_tiles, n_tiles, final):
    t_ = TILE
    fc = D_FF // FF_CHUNKS
    ts, nseg = _segment(pl.program_id(1), n_ctx_tiles, n_tiles)
    n2 = n2_ref[...]
    row = lax.broadcasted_iota(jnp.int32, (t_ + 2 * HALO, 1), 0)
    row_lo = jnp.where(ts > 0, 0, HALO)
    row_hi = jnp.where(ts < nseg - 1, t_ + 2 * HALO, t_ + HALO)

    nb = x_ref.shape[0]

    def normed(sb):
        mod = mod_ref[sb]
        xe = jnp.concatenate([xp_ref[sb], x_ref[sb], xn_ref[sb]], axis=0)
        hf = _rms(xe, n2) * (1.0 + mod[4:5]) + mod[3:4]
        return hf.astype(BF16), hf[HALO:HALO + t_].astype(BF16)

    def up(sb, c, he, h):
        cg = slice(D_FF + c * fc, D_FF + (c + 1) * fc)
        u = _dot(h, wup_ref[:, c * fc:(c + 1) * fc])
        g_ext = _dot(he, wup_ref[:, cg])
        ge_ref[sb] = jnp.where(row >= row_lo, jnp.where(row < row_hi, g_ext, 0.0), 0.0)
        return u, g_ext[HALO:HALO + t_]

    def down(sb, c, u, g):
        cu = slice(c * fc, (c + 1) * fc)
        cw = cw_ref[:, cu]
        gc = (ge_ref[sb, pl.ds(HALO - 1, t_), :] * cw[0:1] + g * cw[1:2]
              + ge_ref[sb, pl.ds(HALO + 1, t_), :] * cw[2:3] + cb_ref[:, cu])
        return _dot((_silu(gc) * u).astype(BF16), wdn_ref[cu, :])

    hs = [None] * nb
    ugs = [None] * nb
    ys = [0.0] * nb
    hs[0] = normed(0)
    for c in range(FF_CHUNKS):
        for sb in range(nb):
            ugs[sb] = up(sb, c, *hs[sb])
            if c == 0 and sb + 1 < nb:
                hs[sb + 1] = normed(sb + 1)
            if sb > 0:
                ys[sb - 1] = ys[sb - 1] + down(sb - 1, c, *ugs[sb - 1])
        ys[nb - 1] = ys[nb - 1] + down(nb - 1, c, *ugs[nb - 1])
    for sb in range(nb):
        x2 = x_ref[sb] + mod_ref[sb][5:6] * ys[sb]
        o_ref[sb] = _rms(x2, nf_ref[...]) if final else x2


def _ffn(x1, mod, n2, wup, cw, cb, wdn, nf, *, n_ctx_tiles, ctx_row, final):
    b_, s_, d = x1.shape
    t_ = TILE
    nt = s_ // t_
    nb = FF_BATCH if b_ % FF_BATCH == 0 and ctx_row % FF_BATCH == 0 else 1
    hp, hn = _halo_specs(d, n_ctx_tiles, nt, 0, nb)
    return pl.pallas_call(
        functools.partial(_ffn_kernel, n_ctx_tiles=n_ctx_tiles, n_tiles=nt, final=final),
        out_shape=jax.ShapeDtypeStruct((b_, s_, d), F32),
        grid=(b_ // nb, nt),
        in_specs=[
            pl.BlockSpec((nb, t_, d), lambda b, t: (b, t, 0)), hp, hn,
            pl.BlockSpec((nb, 6, d), lambda b, t: (jnp.where(t < n_ctx_tiles, ctx_row // nb, b), 0, 0)),
            _const_spec(n2.shape), _const_spec(wup.shape), _const_spec(cw.shape), _const_spec(cb.shape),
            _const_spec(wdn.shape), _const_spec(nf.shape),
        ],
        out_specs=pl.BlockSpec((nb, t_, d), lambda b, t: (b, t, 0)),
        scratch_shapes=[pltpu.VMEM((nb, t_ + 2 * HALO, D_FF // FF_CHUNKS), F32)],
        compiler_params=_params(),
        name="ffn",
    )(x1, x1, x1, mod, n2, wup, cw, cb, wdn, nf)


def _rot_cols(w):
    w4 = w.reshape(w.shape[:-1] + (2, 2, MLA_ROPE // 4))
    return jnp.stack([-w4[..., 1, :], w4[..., 0, :]], axis=-2).reshape(w.shape)


def _prep_weights(w_in, w_uq, w_ukv, w_gk_f, b_gk_f, w_gk_b, b_gk_b, w_pool, gla_norm):
    depth, d, _ = w_in.shape
    z = lambda *s: jnp.zeros((depth,) + s, F32)
    o = 0
    parts = {}
    for name, size in (("ckv", KV_LORA), ("kr", MLA_ROPE), ("gk", GLA_KW), ("gv", GLA_VW),
                       ("lrf", GLA_GATE_RANK), ("lrb", GLA_GATE_RANK), ("pool", POOL_W), ("cq", Q_LORA),
                       ("gq", GLA_KW), ("og", GLA_VW)):
        parts[name] = w_in[..., o:o + size]
        o += size
    win = jnp.concatenate([
        parts["ckv"], parts["cq"], parts["pool"], parts["gq"], parts["gk"], parts["gv"], parts["og"],
        parts["lrf"], parts["lrb"], z(d, 32), parts["kr"], z(d, 32),
        z(d, 64), _rot_cols(parts["kr"]), z(d, 32)], axis=-1).astype(BF16)

    uq = w_uq.reshape(depth, Q_LORA, MLA_HEADS, MLA_QK)
    nope, rope = uq[..., :MLA_NOPE], uq[..., MLA_NOPE:]
    zq = lambda w: jnp.zeros((depth, Q_LORA, MLA_HEADS, w), F32)
    qa = jnp.concatenate([nope, rope, zq(32)], axis=-1).reshape(depth, Q_LORA, -1)
    qb = jnp.concatenate([zq(64), _rot_cols(rope), zq(32)], axis=-1).reshape(depth, Q_LORA, -1)
    wuqt = jnp.swapaxes(jnp.concatenate([qa, qb], axis=-1), 1, 2).astype(BF16)

    ukv = w_ukv.reshape(depth, KV_LORA, MLA_HEADS, MLA_NOPE + MLA_V)
    kn = jnp.concatenate([ukv[..., :MLA_NOPE], jnp.zeros((depth, KV_LORA, MLA_HEADS, 64), F32)], axis=-1)
    wuk = kn.reshape(depth, KV_LORA, -1).astype(BF16)
    wvt = jnp.swapaxes(ukv[..., MLA_NOPE:].reshape(depth, KV_LORA, -1), 1, 2).astype(BF16)

    r = GLA_GATE_RANK
    wg = jnp.zeros((depth, LANES, 2 * GLA_KW), F32)
    wg = wg.at[:, 0:r, 0:GLA_KW].set(w_gk_f).at[:, r:2 * r, GLA_KW:].set(w_gk_b).astype(BF16)
    bg = jnp.concatenate([b_gk_f, b_gk_b], axis=-1)[:, None, :]

    wpool = jnp.zeros((depth, POOL_W, POOL_W), F32)
    for g in range(POOL_GROUPS):
        sl = slice(g * POOL_GC, (g + 1) * POOL_GC)
        wpool = wpool.at[:, sl, sl].set(w_pool[:, g])
    gn = jnp.tile(gla_norm, (1, GLA_HEADS))[:, None, :]
    return win, wuqt, wuk, wvt, wg, bg, wpool.astype(BF16), gn


def _rope_tables(seq, ctx_len):
    rows = seq // GRID_W
    row = jnp.repeat(jnp.arange(rows), GRID_W).astype(F32)
    col = jnp.tile(jnp.arange(GRID_W), rows).astype(F32)
    half = MLA_ROPE // 2
    inv = ROPE_THETA ** (-jnp.arange(0, half, 2, dtype=F32) / half)
    ar = row[:, None] * inv
    ac = col[:, None] * inv
    ang = jnp.concatenate([ar, ar, ac, ac], axis=-1)
    cos = jnp.concatenate([jnp.ones((ctx_len, MLA_ROPE), F32), jnp.cos(ang)], axis=0)
    sin = jnp.concatenate([jnp.zeros((ctx_len, MLA_ROPE), F32), jnp.sin(ang)], axis=0)
    s_ = ctx_len + seq
    scale = MLA_QK ** -0.5 * LOG2E
    z = lambda w: jnp.zeros((s_, w), F32)
    tabk = jnp.concatenate([z(MLA_NOPE), cos, z(32), z(MLA_NOPE), sin, z(32)], axis=-1)
    tabq = jnp.concatenate([jnp.ones((s_, MLA_NOPE), F32) * scale, cos * scale, z(32),
                            z(MLA_NOPE), sin * scale, z(32)], axis=-1)
    return tabk, tabq.T


def kernel(x, c, ctx, c_ctx, w_ada, b_ada, norm1, norm2, w_in, q_norm, w_uq, kv_norm, w_ukv, w_gk_f, b_gk_f,
           w_gk_b, b_gk_b, gla_norm, w_pool, pool_scale, w_o, w_up, conv_w, conv_b, w_down, norm_f):
    b_, seq, d = x.shape
    ctx_len = ctx.shape[1]
    depth = w_ada.shape[0]
    assert seq % TILE == 0 and ctx_len % TILE == 0 and seq % GRID_W == 0
    nc = ctx_len // TILE

    bp = -(-(b_ + FF_BATCH) // 8) * 8
    cvec = jnp.concatenate([c, jnp.tile(c_ctx[None, :], (FF_BATCH, 1)), jnp.zeros((bp - b_ - FF_BATCH, d), F32)],
                           axis=0)
    mod_all = _ada(cvec, w_ada, b_ada).reshape(depth, bp, 6, d)

    win, wuqt, wuk, wvt, wg, bg, wpool, gn = _prep_weights(w_in, w_uq, w_ukv, w_gk_f, b_gk_f, w_gk_b, b_gk_b,
                                                      w_pool, gla_norm)
    wo = w_o.astype(BF16)
    wup = w_up.astype(BF16)
    wdn = w_down.astype(BF16)
    tabk, tabq = _rope_tables(seq, ctx_len)
    gi = np.arange(GLA_VW) // GLA_DV
    ones = jnp.asarray((gi[:, None] == gi[None, :]).astype(np.float32), dtype=BF16)

    xa = jnp.concatenate([ctx, x], axis=1)
    tq = ATTN_TQ if seq % ATTN_TQ == 0 else TILE
    for i in range(depth):
        last = i == depth - 1
        t0 = nc if last else 0
        mod = mod_all[i]
        q, k, v, pu, gqk, gg, gv, og = _pre(xa, mod, norm1[i][None], win[i], q_norm[i][None], wuqt[i],
                                            kv_norm[i][None], wuk[i], wvt[i], wg[i], bg[i], tabk, tabq,
                                            n_ctx_tiles=nc, ctx_row=b_)
        att_l = _attn(q, k, v, tq=tq, q_start=0, n_q=seq // tq, n_keys=ctx_len + seq)
        att_c = att_l if last else _attn(q, k, v, tq=TILE, q_start=seq, n_q=nc, n_keys=ctx_len)
        of, ob = _gla(gqk, gg, gv, n_ctx_tiles=nc)
        x1 = _post(xa, mod, pu, att_c, att_l, of, ob, og, gn[i], wpool[i], pool_scale[i][None], wo[i], ones,
                   n_ctx_tiles=nc, ctx_row=b_, t0=t0)
        xa = _ffn(x1, mod, norm2[i][None], wup[i], conv_w[i], conv_b[i][None], wdn[i], norm_f[None],
                  n_ctx_tiles=0 if last else nc, ctx_row=b_, final=last)
    return xa
```
